```python
import math, functools
import jax, jax.numpy as jnp
from jax import lax
import numpy as np

D_MODEL = 1024
BATCH = 4
SEQ = 8192
DEPTH = 1
DEC_BATCH = 32
DEC_SEQ = 1
PAST_LEN = 16384
PAGE_SIZE = 128

SSD_EXPAND = 2
SSD_INNER = SSD_EXPAND * D_MODEL
SSD_HEADDIM = 64
SSD_HEADS = SSD_INNER // SSD_HEADDIM
SSD_GROUPS = 8
SSD_STATE = 128
SSD_CONV = 4
SSD_CHUNK = 128
SSD_CONV_DIM = SSD_INNER + 2 * SSD_GROUPS * SSD_STATE
ATT_HEADS = 16
ATT_HEADDIM = 64
ATT_WIDTH = ATT_HEADS * ATT_HEADDIM
MOBA_BLOCK = 256
MOBA_TOPK = 3
MOBA_QBLOCK = 32
ROPE_THETA = 10000.0
MEM_TOKENS = 256
MEM_HEADS = 4
MEM_HEADDIM = 256
MEM_WIDTH = MEM_HEADS * MEM_HEADDIM
PEER_HEADS = 8
PEER_NKEYS = 128
PEER_EXPERTS = PEER_NKEYS * PEER_NKEYS
PEER_DKEY = 256
PEER_TOPK = 16
PEER_BLOCK = 128
N_BRANCH = 3
IN_TOTAL = SSD_INNER + SSD_CONV_DIM + SSD_HEADS + 3 * ATT_WIDTH + MEM_WIDTH + N_BRANCH * D_MODEL
EPS = 1e-6

kernel_name = 'hybrid_ssd_moba_mem_peer_step'


def rmsnorm(x, w):
    xf = x.astype(jnp.float32)
    y = xf * lax.rsqrt(jnp.mean(xf * xf, axis=-1, keepdims=True) + EPS)
    return (y * w.astype(jnp.float32)).astype(x.dtype)


def rope(x, pos):
    d = x.shape[-1]
    inv = 1.0 / (ROPE_THETA ** (jnp.arange(0, d, 2, dtype=jnp.float32) / d))
    ang = pos.astype(jnp.float32)[:, None] * inv[None, :]
    cos = jnp.cos(ang)[:, None, :]
    sin = jnp.sin(ang)[:, None, :]
    xf = x.astype(jnp.float32)
    x1, x2 = xf[..., : d // 2], xf[..., d // 2:]
    return jnp.concatenate([x1 * cos - x2 * sin, x1 * sin + x2 * cos], axis=-1).astype(x.dtype)


def split_in(p):
    sizes = (SSD_INNER, SSD_CONV_DIM, SSD_HEADS, ATT_WIDTH, ATT_WIDTH, ATT_WIDTH, MEM_WIDTH, D_MODEL, D_MODEL, D_MODEL)
    out, start = [], 0
    for s in sizes:
        out.append(p[..., start:start + s])
        start += s
    return out


def causal_conv(xbc, prev, w, b):
    l = xbc.shape[1]
    full = jnp.concatenate([prev.astype(xbc.dtype), xbc], axis=1)
    acc = full[:, 0:l] * w[0]
    for j in range(1, SSD_CONV):
        acc = acc + full[:, j:j + l] * w[j]
    return jax.nn.silu(acc + b), full[:, full.shape[1] - (SSD_CONV - 1):]


def ssd_scan(x, dt, a, bm, cm, h0):
    b, l = x.shape[0], x.shape[1]
    L = min(SSD_CHUNK, l)
    pad = (-l) % L
    if pad:
        x = jnp.pad(x, ((0, 0), (0, pad), (0, 0), (0, 0)))
        dt = jnp.pad(dt, ((0, 0), (0, pad), (0, 0)))
        bm = jnp.pad(bm, ((0, 0), (0, pad), (0, 0), (0, 0)))
        cm = jnp.pad(cm, ((0, 0), (0, pad), (0, 0), (0, 0)))
    nc = (l + pad) // L
    G, R, P, N = SSD_GROUPS, SSD_HEADS // SSD_GROUPS, SSD_HEADDIM, SSD_STATE
    xc = x.reshape(b, nc, L, G, R, P)
    dtc = dt.reshape(b, nc, L, G, R)
    bc = bm.reshape(b, nc, L, G, N)
    cc = cm.reshape(b, nc, L, G, N)
    a_cs = jnp.cumsum(dtc * a.reshape(G, R), axis=2)
    xdt = xc * dtc[..., None]
    seg = a_cs[:, :, :, None] - a_cs[:, :, None, :]
    causal = jnp.tril(jnp.ones((L, L), dtype=bool))
    decay = jnp.exp(jnp.where(causal[:, :, None, None], seg, -jnp.inf))
    cb = jnp.einsum('bclgn,bcsgn->bclsg', cc, bc)
    y_diag = jnp.einsum('bclsg,bclsgr,bcsgrp->bclgrp', cb, decay, xdt)
    decay_end = jnp.exp(a_cs[:, :, -1:] - a_cs)
    states = jnp.einsum('bcsgn,bcsgr,bcsgrp->bcgrpn', bc, decay_end, xdt)
    chunk_decay = jnp.exp(a_cs[:, :, -1])

    def step(h, inp):
        s, d = inp
        return h * d[..., None, None] + s, h

    h_last, h_prev = lax.scan(step, h0.reshape(b, G, R, P, N),
                              (jnp.moveaxis(states, 1, 0), jnp.moveaxis(chunk_decay, 1, 0)))
    h_prev = jnp.moveaxis(h_prev, 0, 1)
    y_off = jnp.einsum('bclgn,bcgrpn,bclgr->bclgrp', cc, h_prev, jnp.exp(a_cs))
    y = (y_diag + y_off).reshape(b, nc * L, SSD_HEADS, P)[:, :l]
    return y, h_last.reshape(b, SSD_HEADS, P, N)


def ssd_branch(z, xbc, dt_raw, conv_prev, h0, conv_w, conv_b, dt_bias, a_log, d_skip, norm_w):
    b, l, _ = z.shape
    f32 = jnp.float32
    xbc, conv_new = causal_conv(xbc, conv_prev, conv_w, conv_b)
    gn = SSD_GROUPS * SSD_STATE
    xs = xbc[..., :SSD_INNER].reshape(b, l, SSD_HEADS, SSD_HEADDIM).astype(f32)
    bm = xbc[..., SSD_INNER:SSD_INNER + gn].reshape(b, l, SSD_GROUPS, SSD_STATE).astype(f32)
    cm = xbc[..., SSD_INNER + gn:].reshape(b, l, SSD_GROUPS, SSD_STATE).astype(f32)
    dt = jax.nn.softplus(dt_raw.astype(f32) + dt_bias.astype(f32))
    a = -jnp.exp(a_log.astype(f32))
    y, h = ssd_scan(xs, dt, a, bm, cm, h0.astype(f32))
    y = y + xs * d_skip.astype(f32)[:, None]
    y = y.reshape(b, l, SSD_INNER) * jax.nn.silu(z.astype(f32))
    return rmsnorm(y, norm_w).astype(z.dtype), h.astype(h0.dtype), conv_new


def moba_seq(q, q_pos, k, v):
    lk = k.shape[0]
    nb = -(-lk // MOBA_BLOCK)
    padk = nb * MOBA_BLOCK - lk
    kb = jnp.pad(k, ((0, padk), (0, 0), (0, 0))).reshape(nb, MOBA_BLOCK, ATT_HEADS, ATT_HEADDIM).transpose(2, 0, 1, 3)
    vb = jnp.pad(v, ((0, padk), (0, 0), (0, 0))).reshape(nb, MOBA_BLOCK, ATT_HEADS, ATT_HEADDIM).transpose(2, 0, 1, 3)
    k_mean = kb.astype(jnp.float32).mean(axis=2)
    lq = q.shape[0]
    qb = min(MOBA_QBLOCK, lq)
    padq = (-lq) % qb
    qp = jnp.pad(q, ((0, padq), (0, 0), (0, 0))).reshape(-1, qb, ATT_HEADS, ATT_HEADDIM)
    pp = jnp.pad(q_pos, (0, padq)).reshape(-1, qb)
    kk = min(MOBA_TOPK, nb)
    scale = ATT_HEADDIM ** -0.5
    hidx = jnp.arange(ATT_HEADS)[None, :, None]
    slot = jnp.arange(kk + 1)

    def one_block(args):
        qq, pos = args
        own = pos // MOBA_BLOCK
        gate = jnp.einsum('qhd,hnd->qhn', qq.astype(jnp.float32), k_mean)
        past = jnp.arange(nb)[None, None, :] < own[:, None, None]
        gate = jnp.where(past, gate, -jnp.inf)
        _, top = lax.top_k(gate, kk)
        sel = jnp.concatenate([top, jnp.broadcast_to(own[:, None, None], (qb, ATT_HEADS, 1))], axis=-1)
        ks = kb[hidx, sel]
        vs = vb[hidx, sel]
        s = jnp.einsum('qhd,qhjtd->qhjt', qq, ks).astype(jnp.float32) * scale
        key_pos = sel[..., None] * MOBA_BLOCK + jnp.arange(MOBA_BLOCK)
        slot_ok = jnp.where(slot[None, :] < kk, slot[None, :] < own[:, None], True)
        ok = slot_ok[:, None, :, None] & (key_pos <= pos[:, None, None, None])
        s = jnp.where(ok, s, -jnp.inf)
        p = jax.nn.softmax(s.reshape(qb, ATT_HEADS, -1), axis=-1).reshape(s.shape)
        return jnp.einsum('qhjt,qhjtd->qhd', p.astype(vs.dtype), vs)

    out = lax.map(one_block, (qp, pp))
    return out.reshape(-1, ATT_HEADS, ATT_HEADDIM)[:lq]


def prompt_attention(q, k, v, pos):
    return lax.map(lambda a: moba_seq(a[0], pos, a[1], a[2]), (q, k, v))


def paged_attention(q, k, v, pos, pool_k, pool_v, page_table, layer):
    def one(a):
        qs, ks_new, vs_new, pt = a
        kp = pool_k[layer, pt].reshape(-1, ATT_HEADS, ATT_HEADDIM)
        vp = pool_v[layer, pt].reshape(-1, ATT_HEADS, ATT_HEADDIM)
        kf = jnp.concatenate([kp, ks_new.astype(kp.dtype)], axis=0)
        vf = jnp.concatenate([vp, vs_new.astype(vp.dtype)], axis=0)
        return moba_seq(qs.astype(kp.dtype), pos, kf, vf)
    return lax.map(one, (q, k, v, page_table))


def mem_kv(mem, norm_w, wk, wv, kn_w):
    b, m, _ = mem.shape
    mn = rmsnorm(mem, norm_w)
    k = rmsnorm((mn @ wk).reshape(b, m, MEM_HEADS, MEM_HEADDIM), kn_w)
    v = (mn @ wv).reshape(b, m, MEM_HEADS, MEM_HEADDIM)
    return k, v


def mem_attend(q, k, v):
    s = jnp.einsum('blhd,bmhd->bhlm', q, k.astype(q.dtype)).astype(jnp.float32) * (MEM_HEADDIM ** -0.5)
    p = jax.nn.softmax(s, axis=-1)
    return jnp.einsum('bhlm,bmhd->blhd', p.astype(q.dtype), v.astype(q.dtype))


def peer(h, wq, keys, u, v):
    t = h.shape[0]
    blk = min(PEER_BLOCK, t)
    pad = (-t) % blk
    hp = jnp.pad(h, ((0, pad), (0, 0))).reshape(-1, blk, D_MODEL)

    def one(hb):
        q = (hb @ wq).reshape(blk, PEER_HEADS, 2, PEER_DKEY // 2).astype(jnp.float32)
        s = jnp.einsum('thcd,hckd->thck', q, keys.astype(jnp.float32))
        sv, si = lax.top_k(s, PEER_TOPK)
        cand = sv[:, :, 0, :, None] + sv[:, :, 1, None, :]
        cidx = si[:, :, 0, :, None] * PEER_NKEYS + si[:, :, 1, None, :]
        fv, fi = lax.top_k(cand.reshape(blk, PEER_HEADS, -1), PEER_TOPK)
        eidx = jnp.take_along_axis(cidx.reshape(blk, PEER_HEADS, -1), fi, axis=-1)
        g = jax.nn.softmax(fv, axis=-1)
        ue = u[eidx]
        ve = v[eidx]
        act = jax.nn.gelu(jnp.einsum('td,thkd->thk', hb, ue).astype(jnp.float32), approximate=False)
        return jnp.einsum('thk,thkd->td', (g * act).astype(ve.dtype), ve)

    out = lax.map(one, hp)
    return out.reshape(-1, D_MODEL)[:t]


def mixer_block(x, pos, conv_prev, ssm_prev, mem_k, mem_v, attn_fn, lp):
    b, l, _ = x.shape
    h = rmsnorm(x, lp['norm1_w'])
    z, xbc, dt_raw, q, k, v, mq, g_ssd, g_att, g_mem = split_in(h @ lp['w_in'])
    y_ssd, ssm_new, conv_new = ssd_branch(z, xbc, dt_raw, conv_prev, ssm_prev, lp['conv_w'], lp['conv_b'],
                                          lp['dt_bias'], lp['a_log'], lp['d_skip'], lp['ssd_norm_w'])
    q = rope(rmsnorm(q.reshape(b, l, ATT_HEADS, ATT_HEADDIM), lp['q_norm_w']), pos)
    k = rope(rmsnorm(k.reshape(b, l, ATT_HEADS, ATT_HEADDIM), lp['k_norm_w']), pos)
    v = v.reshape(b, l, ATT_HEADS, ATT_HEADDIM)
    o_att = attn_fn(q, k, v, pos).reshape(b, l, ATT_WIDTH).astype(x.dtype)
    mq = rmsnorm(mq.reshape(b, l, MEM_HEADS, MEM_HEADDIM), lp['mem_qn_w'])
    o_mem = mem_attend(mq, mem_k, mem_v).reshape(b, l, MEM_WIDTH)
    merged = (jax.nn.sigmoid(g_ssd) * (y_ssd @ lp['w_br_ssd'])
              + jax.nn.sigmoid(g_att) * (o_att @ lp['w_br_attn'])
              + jax.nn.sigmoid(g_mem) * (o_mem @ lp['w_br_mem']))
    x = x + merged @ lp['w_out']
    hf = rmsnorm(x, lp['norm2_w']).reshape(b * l, D_MODEL)
    x = x + peer(hf, lp['peer_wq'], lp['peer_keys'], lp['peer_u'], lp['peer_v']).reshape(b, l, D_MODEL)
    return x, k, v, ssm_new, conv_new


def setup_inputs(seed: int = 0) -> dict:
    key = jax.random.key(seed)
    ks = jax.random.split(key, 40)
    f32 = jnp.float32
    n_pages = PAST_LEN // PAGE_SIZE
    used = DEC_BATCH * n_pages
    n_pool = used + max(1, used // 4)

    def nrm(k, shape, s):
        return jax.random.normal(k, shape, f32) * s

    def gain(k, shape):
        return 1.0 + nrm(k, shape, 0.02)

    dt0 = jnp.exp(jax.random.uniform(ks[14], (DEPTH, SSD_HEADS), f32, math.log(1e-3), math.log(1e-1)))
    return {
        'x_prompt': nrm(ks[0], (BATCH, SEQ, D_MODEL), 1.0),
        'x_sample': nrm(ks[1], (DEC_BATCH, DEC_SEQ, D_MODEL), 1.0),
        'mem_prompt': nrm(ks[2], (BATCH, MEM_TOKENS, D_MODEL), 1.0),
        'cache_k': nrm(ks[3], (DEPTH, n_pool, PAGE_SIZE, ATT_HEADS, ATT_HEADDIM), 1.0),
        'cache_v': nrm(ks[4], (DEPTH, n_pool, PAGE_SIZE, ATT_HEADS, ATT_HEADDIM), 1.0),
        'page_table': jax.random.permutation(ks[5], n_pool)[:used].reshape(DEC_BATCH, n_pages).astype(jnp.int32),
        'cache_mem_k': nrm(ks[6], (DEPTH, DEC_BATCH, MEM_TOKENS, MEM_HEADS, MEM_HEADDIM), 1.0),
        'cache_mem_v': nrm(ks[7], (DEPTH, DEC_BATCH, MEM_TOKENS, MEM_HEADS, MEM_HEADDIM), 1.0),
        'state_ssm': nrm(ks[8], (DEPTH, DEC_BATCH, SSD_HEADS, SSD_HEADDIM, SSD_STATE), 0.1),
        'state_conv': nrm(ks[9], (DEPTH, DEC_BATCH, SSD_CONV - 1, SSD_CONV_DIM), 1.0),
        'norm1_w': gain(ks[10], (DEPTH, D_MODEL)),
        'w_in': nrm(ks[11], (DEPTH, D_MODEL, IN_TOTAL), D_MODEL ** -0.5),
        'conv_w': nrm(ks[12], (DEPTH, SSD_CONV, SSD_CONV_DIM), SSD_CONV ** -0.5),
        'conv_b': nrm(ks[13], (DEPTH, SSD_CONV_DIM), 0.02),
        'dt_bias': dt0 + jnp.log(-jnp.expm1(-dt0)),
        'a_log': jnp.log(jax.random.uniform(ks[15], (DEPTH, SSD_HEADS), f32, 1.0, 16.0)),
        'd_skip': gain(ks[16], (DEPTH, SSD_HEADS)),
        'ssd_norm_w': gain(ks[17], (DEPTH, SSD_INNER)),
        'q_norm_w': gain(ks[18], (DEPTH, ATT_HEADDIM)),
        'k_norm_w': gain(ks[19], (DEPTH, ATT_HEADDIM)),
        'mem_norm_w': gain(ks[20], (DEPTH, D_MODEL)),
        'w_mem_k': nrm(ks[21], (DEPTH, D_MODEL, MEM_WIDTH), D_MODEL ** -0.5),
        'w_mem_v': nrm(ks[22], (DEPTH, D_MODEL, MEM_WIDTH), D_MODEL ** -0.5),
        'mem_qn_w': gain(ks[23], (DEPTH, MEM_HEADDIM)),
        'mem_kn_w': gain(ks[24], (DEPTH, MEM_HEADDIM)),
        'w_br_ssd': nrm(ks[25], (DEPTH, SSD_INNER, D_MODEL), SSD_INNER ** -0.5),
        'w_br_attn': nrm(ks[26], (DEPTH, ATT_WIDTH, D_MODEL), ATT_WIDTH ** -0.5),
        'w_br_mem': nrm(ks[27], (DEPTH, MEM_WIDTH, D_MODEL), MEM_WIDTH ** -0.5),
        'w_out': nrm(ks[28], (DEPTH, D_MODEL, D_MODEL), D_MODEL ** -0.5),
        'norm2_w': gain(ks[29], (DEPTH, D_MODEL)),
        'peer_wq': nrm(ks[30], (DEPTH, D_MODEL, PEER_HEADS * PEER_DKEY), D_MODEL ** -0.5),
        'peer_keys': nrm(ks[31], (DEPTH, PEER_HEADS, 2, PEER_NKEYS, PEER_DKEY // 2), (PEER_DKEY // 2) ** -0.5),
        'peer_u': nrm(ks[32], (DEPTH, PEER_EXPERTS, D_MODEL), D_MODEL ** -0.5),
        'peer_v': nrm(ks[33], (DEPTH, PEER_EXPERTS, D_MODEL), 0.3),
    }


def reference(x_prompt, x_sample, mem_prompt, cache_k, cache_v, page_table, cache_mem_k, cache_mem_v,
              state_ssm, state_conv, norm1_w, w_in, conv_w, conv_b, dt_bias, a_log, d_skip, ssd_norm_w,
              q_norm_w, k_norm_w, mem_norm_w, w_mem_k, w_mem_v, mem_qn_w, mem_kn_w, w_br_ssd, w_br_attn,
              w_br_mem, w_out, norm2_w, peer_wq, peer_keys, peer_u, peer_v):
    y_p, y_s = x_prompt, x_sample
    bp, lp_len = x_prompt.shape[0], x_prompt.shape[1]
    pos_p = jnp.arange(lp_len, dtype=jnp.int32)
    pos_s = PAST_LEN + jnp.arange(x_sample.shape[1], dtype=jnp.int32)
    kp_l, vp_l, mkp_l, mvp_l, ssmp_l, convp_l = [], [], [], [], [], []
    ks_l, vs_l, ssms_l, convs_l = [], [], [], []
    for i in range(DEPTH):
        lp = dict(norm1_w=norm1_w[i], w_in=w_in[i], conv_w=conv_w[i], conv_b=conv_b[i], dt_bias=dt_bias[i],
                  a_log=a_log[i], d_skip=d_skip[i], ssd_norm_w=ssd_norm_w[i], q_norm_w=q_norm_w[i],
                  k_norm_w=k_norm_w[i], mem_qn_w=mem_qn_w[i], w_br_ssd=w_br_ssd[i], w_br_attn=w_br_attn[i],
                  w_br_mem=w_br_mem[i], w_out=w_out[i], norm2_w=norm2_w[i], peer_wq=peer_wq[i],
                  peer_keys=peer_keys[i], peer_u=peer_u[i], peer_v=peer_v[i])
        mk, mv = mem_kv(mem_prompt, mem_norm_w[i], w_mem_k[i], w_mem_v[i], mem_kn_w[i])
        conv0 = jnp.zeros((bp, SSD_CONV - 1, SSD_CONV_DIM), x_prompt.dtype)
        ssm0 = jnp.zeros((bp, SSD_HEADS, SSD_HEADDIM, SSD_STATE), x_prompt.dtype)
        y_p, kp, vp, ssm_p, conv_p = mixer_block(y_p, pos_p, conv0, ssm0, mk, mv, prompt_attention, lp)
        attn_s = functools.partial(paged_attention, pool_k=cache_k, pool_v=cache_v, page_table=page_table, layer=i)
        y_s, kn, vn, ssm_s, conv_s = mixer_block(y_s, pos_s, state_conv[i], state_ssm[i], cache_mem_k[i],
                                                 cache_mem_v[i], attn_s, lp)
        kp_l.append(kp)
        vp_l.append(vp)
        mkp_l.append(mk)
        mvp_l.append(mv)
        ssmp_l.append(ssm_p)
        convp_l.append(conv_p)
        ks_l.append(kn)
        vs_l.append(vn)
        ssms_l.append(ssm_s)
        convs_l.append(conv_s)
    return (y_p, y_s, jnp.stack(kp_l), jnp.stack(vp_l), jnp.stack(mkp_l), jnp.stack(mvp_l),
            jnp.stack(ssmp_l), jnp.stack(convp_l), jnp.stack(ks_l), jnp.stack(vs_l),
            jnp.stack(ssms_l), jnp.stack(convs_l))
```

```python
import functools
import math

import jax
import jax.numpy as jnp
from jax import lax
from jax.experimental import pallas as pl
from jax.experimental.pallas import tpu as pltpu

F32 = jnp.float32
BF16 = jnp.bfloat16
I32 = jnp.int32
EPS = 1e-6
NEG_INF = float("-inf")

D_MODEL = 1024
SSD_INNER = 2048
SSD_HEADS = 32
SSD_HEADDIM = 64
SSD_GROUPS = 8
SSD_STATE = 128
SSD_CONV = 4
SSD_CHUNK = 128
SSD_CONV_DIM = 4096
ATT_HEADS = 16
ATT_HEADDIM = 64
MOBA_BLOCK = 256
MOBA_TOPK = 3
ROPE_THETA = 10000.0
MEM_HEADS = 4
MEM_HEADDIM = 256
PEER_HEADS = 8
PEER_NKEYS = 128
PEER_TOPK = 16
PAST_LEN = 16384
PAGE_SIZE = 128

LANES = 128
SUBLANES = 8
VMEM_LIMIT = 56 * 1024 * 1024

COL_Z, COL_X, COL_BC, COL_Q, COL_K, COL_V, COL_MQ, COL_G = 0, 2, 4, 6, 7, 8, 9, 10
N_MAIN = 13 * 1024


def _params(sem):
    return pltpu.CompilerParams(dimension_semantics=sem, vmem_limit_bytes=VMEM_LIMIT)


def _dot(a, b):
    return jnp.dot(a, b, preferred_element_type=F32)


def _dot_nt(a, b):
    return lax.dot_general(a, b, (((1,), (1,)), ((), ())), preferred_element_type=F32)


def _dot_tn(a, b):
    return lax.dot_general(a, b, (((0,), (0,)), ((), ())), preferred_element_type=F32)


def _split(a):
    hi = a.astype(BF16)
    lo = (a - hi.astype(F32)).astype(BF16)
    return hi, lo


def _dot3(a, b, dot):
    ah, al = _split(a)
    bh, bl = _split(b)
    return dot(ah, bh) + (dot(ah, bl) + dot(al, bh))


def _sigmoid(x):
    return 1.0 / (1.0 + jnp.exp(-x))


def _iota(shape, axis):
    return lax.broadcasted_iota(I32, shape, axis)


def _proj_kernel(x_ref, nw_ref, w_ref, o_ref, h_scr):
    @pl.when(pl.program_id(1) == 0)
    def _():
        x = x_ref[...]
        ms = jnp.mean(x * x, axis=-1, keepdims=True)
        h_scr[...] = (x * lax.rsqrt(ms + EPS) * nw_ref[...]).astype(BF16)

    o_ref[...] = _dot(h_scr[...], w_ref[...])


def _proj(x, nw, w, tm, tn):
    t, d = x.shape
    n = w.shape[1]
    return pl.pallas_call(
        _proj_kernel,
        grid=(t // tm, n // tn),
        in_specs=[pl.BlockSpec((tm, d), lambda i, j: (i, 0)),
                  pl.BlockSpec((1, d), lambda i, j: (0, 0)),
                  pl.BlockSpec((d, tn), lambda i, j: (0, j))],
        out_specs=pl.BlockSpec((tm, tn), lambda i, j: (i, j)),
        out_shape=jax.ShapeDtypeStruct((t, n), F32),
        scratch_shapes=[pltpu.VMEM((tm, d), BF16)],
        compiler_params=_params(("parallel", "arbitrary")),
        name="proj",
    )(x, nw, w)


def _ssd_kernel(z_ref, x_ref, bc_ref, dt_ref, cprev_ref, h0_ref, cw_ref, cb_ref, dtb_ref, alog_ref,
                dsk_ref, nw_ref, y_ref, hl_ref, buf, st, ycat, *, rows):
    L = SSD_CHUNK
    c = pl.program_id(1)

    @pl.when(c == 0)
    def _():
        buf[0:SUBLANES, :] = cprev_ref[0]
        st[...] = h0_ref[0]

    if rows < L:
        buf[SUBLANES:, :] = jnp.zeros((L, SSD_CONV_DIM), F32)
    buf[SUBLANES:SUBLANES + rows, 0:SSD_INNER] = x_ref[0]
    buf[SUBLANES:SUBLANES + rows, SSD_INNER:] = bc_ref[0]

    acc = buf[SUBLANES:SUBLANES + L, :] * cw_ref[SSD_CONV - 1:SSD_CONV, :]
    for j in range(SSD_CONV - 1):
        off = SUBLANES - (SSD_CONV - 1) + j
        acc = acc + buf[off:off + L, :] * cw_ref[j:j + 1, :]
    acc = acc + cb_ref[...]
    xbc = acc * _sigmoid(acc)
    buf[0:SUBLANES, :] = buf[L:L + SUBLANES, :]

    xs = xbc[:, :SSD_INNER]
    n_gn = SSD_GROUPS * SSD_STATE
    bm = xbc[:, SSD_INNER:SSD_INNER + n_gn].astype(BF16)
    cm = xbc[:, SSD_INNER + n_gn:].astype(BF16)

    dt_raw = dt_ref[0] if rows == L else jnp.broadcast_to(dt_ref[0], (L, LANES))
    pre = dt_raw + dtb_ref[...]
    dt = jnp.maximum(pre, 0.0) + jnp.log1p(jnp.exp(-jnp.abs(pre)))
    row_id = _iota((L, LANES), 0)
    dt = jnp.where(row_id < rows, dt, 0.0)
    a = -jnp.exp(alog_ref[...])
    da = dt * a
    tri = _iota((L, L), 0) >= _iota((L, L), 1)
    a_cs = jnp.dot(tri.astype(F32), da, preferred_element_type=F32, precision=lax.Precision.HIGHEST)
    a_cs_t = a_cs.T
    a_last = a_cs[L - 1:L, :]
    e_cs = jnp.exp(a_cs)
    d_end = jnp.exp(a_last - a_cs)
    e_last = jnp.exp(a_last)

    for g in range(SSD_GROUPS):
        bg = bm[:, g * SSD_STATE:(g + 1) * SSD_STATE]
        cg = cm[:, g * SSD_STATE:(g + 1) * SSD_STATE]
        cb = _dot_nt(cg, bg)
        for r in range(SSD_HEADS // SSD_GROUPS):
            h = g * (SSD_HEADS // SSD_GROUPS) + r
            hs = slice(h * SSD_HEADDIM, (h + 1) * SSD_HEADDIM)
            col = a_cs[:, h:h + 1]
            row = a_cs_t[h:h + 1, :]
            decay = jnp.where(tri, jnp.exp(col - row), 0.0)
            m = (cb * decay).astype(BF16)
            xh = xs[:, hs]
            xdt = xh * dt[:, h:h + 1]
            yd = _dot(m, xdt.astype(BF16))
            hp = st[h]
            yo = _dot_nt(cg, hp.astype(BF16)) * e_cs[:, h:h + 1]
            xw = (xdt * d_end[:, h:h + 1]).astype(BF16)
            st[h] = hp * e_last[:, h:h + 1] + _dot_tn(xw, bg)
            ycat[:, hs] = yd + yo + xh * dsk_ref[:, hs]

    z = z_ref[0]
    if rows < L:
        y = ycat[0:rows, :] * (z * _sigmoid(z))
    else:
        y = ycat[...] * (z * _sigmoid(z))
    ms = jnp.mean(y * y, axis=-1, keepdims=True)
    y_ref[0] = (y * lax.rsqrt(ms + EPS) * nw_ref[...]).astype(BF16)

    @pl.when(c == pl.num_programs(1) - 1)
    def _():
        hl_ref[0] = st[...]


def _ssd(p3, dt3, conv_prev8, h0, cw, cb, dtb, alog, dsk, nw, rows):
    nb, s, _ = p3.shape
    nc = s // rows
    w2 = SSD_INNER
    const = lambda shape: pl.BlockSpec(shape, lambda b, c: tuple(0 for _ in shape))
    return pl.pallas_call(
        functools.partial(_ssd_kernel, rows=rows),
        grid=(nb, nc),
        in_specs=[pl.BlockSpec((1, rows, w2), lambda b, c: (b, c, COL_Z // 2)),
                  pl.BlockSpec((1, rows, w2), lambda b, c: (b, c, COL_X // 2)),
                  pl.BlockSpec((1, rows, w2), lambda b, c: (b, c, COL_BC // 2)),
                  pl.BlockSpec((1, rows, LANES), lambda b, c: (b, c, 0)),
                  pl.BlockSpec((1, SUBLANES, SSD_CONV_DIM), lambda b, c: (b, 0, 0)),
                  pl.BlockSpec((1, SSD_HEADS, SSD_HEADDIM, SSD_STATE), lambda b, c: (b, 0, 0, 0)),
                  const((SSD_CONV, SSD_CONV_DIM)), const((1, SSD_CONV_DIM)), const((1, LANES)),
                  const((1, LANES)), const((1, SSD_INNER)), const((1, SSD_INNER))],
        out_specs=[pl.BlockSpec((1, rows, w2), lambda b, c: (b, c, 0)),
                   pl.BlockSpec((1, SSD_HEADS, SSD_HEADDIM, SSD_STATE), lambda b, c: (b, 0, 0, 0))],
        out_shape=[jax.ShapeDtypeStruct((nb, s, w2), BF16),
                   jax.ShapeDtypeStruct((nb, SSD_HEADS, SSD_HEADDIM, SSD_STATE), F32)],
        scratch_shapes=[pltpu.VMEM((SUBLANES + SSD_CHUNK, SSD_CONV_DIM), F32),
                        pltpu.VMEM((SSD_HEADS, SSD_HEADDIM, SSD_STATE), F32),
                        pltpu.VMEM((SSD_CHUNK, SSD_INNER), F32)],
        compiler_params=_params(("parallel", "arbitrary")),
        name="ssd",
    )(p3, p3, p3, dt3, conv_prev8, h0, cw, cb, dtb, alog, dsk, nw)


def _rope_tables(pos):
    d = ATT_HEADDIM
    inv = 1.0 / (ROPE_THETA ** (jnp.arange(0, d, 2, dtype=F32) / d))
    ang = pos.astype(F32)[:, None] * inv[None, :]
    cos = jnp.cos(ang)
    sin = jnp.sin(ang)
    cos = jnp.concatenate([cos, cos, cos, cos], axis=-1)
    sin = jnp.concatenate([-sin, sin, -sin, sin], axis=-1)
    return cos, sin


def _head_norm_rope(x, w, cos, sin, bd, first):
    hi, lo = _split(x * x)
    ms = _dot(hi, bd) + _dot(lo, bd)
    xn = x * lax.rsqrt(ms + EPS) * w
    partner = jnp.where(first, pltpu.roll(xn, LANES - ATT_HEADDIM // 2, 1), pltpu.roll(xn, ATT_HEADDIM // 2, 1))
    return xn * cos + partner * sin


def _head_consts(rows):
    r = _iota((LANES, LANES), 0) // ATT_HEADDIM
    c = _iota((LANES, LANES), 1) // ATT_HEADDIM
    bd = jnp.where(r == c, 1.0 / ATT_HEADDIM, 0.0).astype(BF16)
    first = (_iota((rows, LANES), 1) % ATT_HEADDIM) < (ATT_HEADDIM // 2)
    return bd, first


def _qk_kernel(q_ref, k_ref, v_ref, cos_ref, sin_ref, qw_ref, kw_ref,
               qa_ref, kr_ref, ka_ref, vt_ref, sel_ref, km_scr, qr_scr, *, nblk):
    tb = MOBA_BLOCK
    i = pl.program_id(1)

    @pl.when(i == 0)
    def _():
        km_scr[...] = jnp.zeros_like(km_scr)

    cos = cos_ref[...]
    sin = sin_ref[...]
    bd, first = _head_consts(tb)
    for cb in range(D_MODEL // LANES):
        sl = slice(cb * LANES, (cb + 1) * LANES)
        qr = _head_norm_rope(q_ref[0, :, sl], qw_ref[...], cos, sin, bd, first)
        kr = _head_norm_rope(k_ref[0, :, sl], kw_ref[...], cos, sin, bd, first)
        qr_scr[:, sl] = qr
        qa_ref[0, :, sl] = (qr * (ATT_HEADDIM ** -0.5)).astype(BF16)
        kr_ref[0, :, sl] = kr
        ka_ref[0, :, sl] = kr.astype(BF16)
        k_mean = jnp.sum(kr, axis=0, keepdims=True) * (1.0 / tb)
        km_scr[:, sl] = jnp.where(_iota((nblk, LANES), 0) == i, k_mean, km_scr[:, sl])
    vt_ref[0, 0] = v_ref[0].T.astype(BF16)

    km = km_scr[...]
    lane_head = _iota((nblk, D_MODEL), 1) // ATT_HEADDIM
    kmt = jnp.concatenate([jnp.where(lane_head == h, km, 0.0) for h in range(ATT_HEADS)], axis=0)
    gate = _dot3(kmt, qr_scr[...], _dot_nt)
    n_id = _iota((nblk, tb), 0)
    valid = n_id < i
    for h in range(ATT_HEADS):
        g = jnp.where(valid, gate[h * nblk:(h + 1) * nblk, :], NEG_INF)
        chosen = jnp.zeros((nblk, tb), F32)
        for _ in range(MOBA_TOPK):
            m = jnp.max(g, axis=0, keepdims=True)
            idx = jnp.min(jnp.where(g == m, n_id, nblk), axis=0, keepdims=True)
            pick = n_id == idx
            chosen = jnp.where(pick & valid, 1.0, chosen)
            g = jnp.where(pick, NEG_INF, g)
        sel_ref[0, h] = chosen


def _qk(p3, cos, sin, qw, kw):
    nb, s, _ = p3.shape
    tb = MOBA_BLOCK
    nblk = s // tb
    col = lambda cidx: pl.BlockSpec((1, tb, D_MODEL), lambda b, i: (b, i, cidx))
    out_tok = pl.BlockSpec((1, tb, D_MODEL), lambda b, i: (b, i, 0))
    return pl.pallas_call(
        functools.partial(_qk_kernel, nblk=nblk),
        grid=(nb, nblk),
        in_specs=[col(COL_Q), col(COL_K), col(COL_V),
                  pl.BlockSpec((tb, LANES), lambda b, i: (i, 0)),
                  pl.BlockSpec((tb, LANES), lambda b, i: (i, 0)),
                  pl.BlockSpec((1, LANES), lambda b, i: (0, 0)),
                  pl.BlockSpec((1, LANES), lambda b, i: (0, 0))],
        out_specs=[out_tok, out_tok, out_tok,
                   pl.BlockSpec((1, 1, D_MODEL, tb), lambda b, i: (b, i, 0, 0)),
                   pl.BlockSpec((1, ATT_HEADS, nblk, tb), lambda b, i: (b, 0, 0, i))],
        out_shape=[jax.ShapeDtypeStruct((nb, s, D_MODEL), BF16),
                   jax.ShapeDtypeStruct((nb, s, D_MODEL), F32),
                   jax.ShapeDtypeStruct((nb, s, D_MODEL), BF16),
                   jax.ShapeDtypeStruct((nb, nblk, D_MODEL, tb), BF16),
                   jax.ShapeDtypeStruct((nb, ATT_HEADS, nblk, s), F32)],
        scratch_shapes=[pltpu.VMEM((nblk, D_MODEL), F32), pltpu.VMEM((tb, D_MODEL), F32)],
        compiler_params=_params(("parallel", "arbitrary")),
        name="qk",
    )(p3, p3, p3, cos, sin, qw, kw)


def _attn_kernel(q_ref, k_ref, vt_ref, sel_ref, o_ref):
    tb = MOBA_BLOCK
    hd = ATT_HEADDIM
    i = pl.program_id(2)
    q = q_ref[0]
    lane_head = _iota((tb, LANES), 1) // hd
    causal = _iota((tb, tb), 0) <= _iota((tb, tb), 1)
    outs = []
    for hh in range(LANES // hd):
        qh = jnp.where(lane_head == hh, q, jnp.zeros_like(q))
        start = pl.multiple_of(i * tb, tb)
        s = jnp.where(causal, _dot_nt(k_ref[0, pl.ds(start, tb), :], qh), NEG_INF)
        m = jnp.max(s, axis=0, keepdims=True)
        p = jnp.exp(s - m)
        l = jnp.sum(p, axis=0, keepdims=True)
        acc = _dot(vt_ref[0, i, hh * hd:(hh + 1) * hd, :], p.astype(BF16))

        def body(j, carry, qh=qh, hh=hh):
            m, l, acc = carry
            s = _dot_nt(k_ref[0, pl.ds(pl.multiple_of(j * tb, tb), tb), :], qh)
            on = sel_ref[0, hh, pl.ds(j, 1), :] > 0.5
            m_new = jnp.where(on, jnp.maximum(m, jnp.max(s, axis=0, keepdims=True)), m)
            p = jnp.where(on, jnp.exp(s - m_new), 0.0)
            alpha = jnp.exp(m - m_new)
            l = alpha * l + jnp.sum(p, axis=0, keepdims=True)
            acc = acc * alpha + _dot(vt_ref[0, j, hh * hd:(hh + 1) * hd, :], p.astype(BF16))
            return m_new, l, acc

        m, l, acc = lax.fori_loop(0, i, body, (m, l, acc))
        outs.append(acc / l)
    o_ref[0] = jnp.concatenate(outs, axis=0).T.astype(BF16)


def _attn(qa, ka, vt, sel):
    nb, s, _ = qa.shape
    tb = MOBA_BLOCK
    nblk = s // tb
    npair = D_MODEL // LANES
    hp = LANES // ATT_HEADDIM
    return pl.pallas_call(
        _attn_kernel,
        grid=(nb, npair, nblk),
        in_specs=[pl.BlockSpec((1, tb, LANES), lambda b, p, i: (b, i, p)),
                  pl.BlockSpec((1, s, LANES), lambda b, p, i: (b, 0, p)),
                  pl.BlockSpec((1, nblk, LANES, tb), lambda b, p, i: (b, 0, p, 0)),
                  pl.BlockSpec((1, hp, nblk, tb), lambda b, p, i: (b, p, 0, i))],
        out_specs=pl.BlockSpec((1, tb, LANES), lambda b, p, i: (b, i, p)),
        out_shape=jax.ShapeDtypeStruct((nb, s, D_MODEL), BF16),
        compiler_params=_params(("parallel", "parallel", "arbitrary")),
        name="attn",
    )(qa, ka, vt, sel)


def _qk_s_kernel(q_ref, k_ref, cos_ref, sin_ref, qw_ref, kw_ref, qr_ref, kr_ref):
    rows = q_ref.shape[0]
    cos = cos_ref[...]
    sin = sin_ref[...]
    bd, first = _head_consts(rows)
    for cb in range(D_MODEL // LANES):
        sl = slice(cb * LANES, (cb + 1) * LANES)
        qr_ref[:, sl] = _head_norm_rope(q_ref[:, sl], qw_ref[...], cos, sin, bd, first)
        kr_ref[:, sl] = _head_norm_rope(k_ref[:, sl], kw_ref[...], cos, sin, bd, first)


def _qk_s(p2, cos, sin, qw, kw):
    n = p2.shape[0]
    col = lambda cidx: pl.BlockSpec((n, D_MODEL), lambda i: (0, cidx))
    full = lambda r: pl.BlockSpec((r, LANES), lambda i: (0, 0))
    return pl.pallas_call(
        _qk_s_kernel,
        grid=(1,),
        in_specs=[col(COL_Q), col(COL_K), full(n), full(n), full(1), full(1)],
        out_specs=[pl.BlockSpec((n, D_MODEL), lambda i: (0, 0))] * 2,
        out_shape=[jax.ShapeDtypeStruct((n, D_MODEL), F32)] * 2,
        compiler_params=_params(("arbitrary",)),
        name="qk_s",
    )(p2, p2, cos, sin, qw, kw)


def _dec_kernel(pt_ref, q_ref, kn_ref, vn_ref, k0_ref, k1_ref, v0_ref, v1_ref, o_ref,
                s_scr, gate_scr, qrow_scr, qa_scr, acc_scr, ml_scr, *, nblk):
    del pt_ref
    tb = MOBA_BLOCK
    j = pl.program_id(1)
    lane_head = _iota((LANES, D_MODEL), 1) // ATT_HEADDIM
    head_rows = lane_head == _iota((LANES, D_MODEL), 0)

    @pl.when(j == 0)
    def _():
        q_rows = jnp.where(head_rows, q_ref[0], 0.0)
        qrow_scr[...] = q_rows
        qa_scr[...] = (q_rows * (ATT_HEADDIM ** -0.5)).astype(BF16)
        gate_scr[...] = jnp.zeros_like(gate_scr)

    @pl.when(j < nblk)
    def _():
        kb = jnp.concatenate([k0_ref[...], k1_ref[...]], axis=0)
        s_scr[j] = _dot_nt(kb.astype(BF16), qa_scr[...])
        km = jnp.sum(kb, axis=0, keepdims=True) * (1.0 / tb)
        gate = _dot3(jnp.broadcast_to(km, (SUBLANES, D_MODEL)), qrow_scr[...], _dot_nt)
        gate_scr[...] = jnp.where(_iota((nblk, LANES), 0) == j, gate[0:1, :], gate_scr[...])

    @pl.when(j == nblk)
    def _():
        g = gate_scr[...]
        n_id = _iota((nblk, LANES), 0)
        chosen = jnp.zeros((nblk, LANES), F32)
        for _ in range(MOBA_TOPK):
            m = jnp.max(g, axis=0, keepdims=True)
            idx = jnp.min(jnp.where(g == m, n_id, nblk), axis=0, keepdims=True)
            pick = n_id == idx
            chosen = jnp.where(pick, 1.0, chosen)
            g = jnp.where(pick, NEG_INF, g)
        kn = jnp.broadcast_to(kn_ref[0], (SUBLANES, D_MODEL)).astype(BF16)
        s_new = _dot_nt(kn, qa_scr[...])[0:1, :]
        masked = jnp.where(chosen[:, None, :] > 0.5, s_scr[...], NEG_INF)
        m = jnp.maximum(jnp.max(jnp.max(masked, axis=1), axis=0, keepdims=True), s_new)
        p = jnp.exp(masked - m[None])
        s_scr[...] = p
        p_new = jnp.exp(s_new - m)
        ml_scr[0:1, :] = p_new
        ml_scr[1:2, :] = jnp.sum(jnp.sum(p, axis=1), axis=0, keepdims=True) + p_new
        acc_scr[...] = jnp.zeros_like(acc_scr)

    @pl.when(j >= nblk)
    def _():
        p = s_scr[j - nblk].astype(BF16)
        p_lanes = _dot(p, jnp.where(head_rows, 1.0, 0.0).astype(BF16))
        vb = jnp.concatenate([v0_ref[...], v1_ref[...]], axis=0).astype(BF16).astype(F32)
        acc_scr[...] += jnp.sum(p_lanes * vb, axis=0, keepdims=True)

    @pl.when(j == 2 * nblk - 1)
    def _():
        ml = jnp.concatenate([ml_scr[...], jnp.zeros((SUBLANES - 2, LANES), F32)], axis=0)
        ml_lanes = jnp.dot(ml, jnp.where(head_rows, 1.0, 0.0), preferred_element_type=F32,
                           precision=lax.Precision.HIGHEST)
        vn = vn_ref[0].astype(BF16).astype(F32)
        o_ref[0] = (acc_scr[...] + ml_lanes[0:1, :] * vn) / ml_lanes[1:2, :]


def _dec_attn(q_rot, k_new, v_new, pool_k, pool_v, page_table):
    nb = q_rot.shape[0]
    n_pages = page_table.shape[1]
    per = MOBA_BLOCK // PAGE_SIZE
    nblk = n_pages // per
    tok = pl.BlockSpec((1, 1, D_MODEL), lambda b, j, pt: (b, 0, 0))

    def page(is_v, half):
        def imap(b, j, pt):
            blk = jnp.maximum(j - nblk, 0) if is_v else jnp.minimum(j, nblk - 1)
            return (pt[b, per * blk + half], 0, 0)
        return pl.BlockSpec((None, PAGE_SIZE, D_MODEL), imap)

    grid_spec = pltpu.PrefetchScalarGridSpec(
        num_scalar_prefetch=1,
        grid=(nb, 2 * nblk),
        in_specs=[tok, tok, tok, page(False, 0), page(False, 1), page(True, 0), page(True, 1)],
        out_specs=pl.BlockSpec((1, 1, D_MODEL), lambda b, j, pt: (b, 0, 0)),
        scratch_shapes=[pltpu.VMEM((nblk, MOBA_BLOCK, LANES), F32),
                        pltpu.VMEM((nblk, LANES), F32),
                        pltpu.VMEM((LANES, D_MODEL), F32),
                        pltpu.VMEM((LANES, D_MODEL), BF16),
                        pltpu.VMEM((1, D_MODEL), F32),
                        pltpu.VMEM((2, LANES), F32)])
    return pl.pallas_call(
        functools.partial(_dec_kernel, nblk=nblk),
        grid_spec=grid_spec,
        out_shape=jax.ShapeDtypeStruct((nb, 1, D_MODEL), F32),
        compiler_params=_params(("parallel", "arbitrary")),
        name="dec_attn",
    )(page_table, q_rot, k_new, v_new, pool_k, pool_k, pool_v, pool_v)


def _headnorm_kernel(x_ref, w_ref, o_ref):
    for h in range(MEM_HEADS):
        sl = slice(h * MEM_HEADDIM, (h + 1) * MEM_HEADDIM)
        x = x_ref[:, sl]
        ms = jnp.mean(x * x, axis=-1, keepdims=True)
        o_ref[:, sl] = x * lax.rsqrt(ms + EPS) * w_ref[...]


def _headnorm(x2, w, tm):
    t = x2.shape[0]
    return pl.pallas_call(
        _headnorm_kernel,
        grid=(t // tm,),
        in_specs=[pl.BlockSpec((tm, D_MODEL), lambda i: (i, 0)), pl.BlockSpec((1, MEM_HEADDIM), lambda i: (0, 0))],
        out_specs=pl.BlockSpec((tm, D_MODEL), lambda i: (i, 0)),
        out_shape=jax.ShapeDtypeStruct((t, D_MODEL), F32),
        compiler_params=_params(("parallel",)),
        name="headnorm",
    )(x2, w)


def _mem_kernel(q_ref, k_ref, v_ref, qw_ref, o_ref):
    kb = k_ref[0].astype(BF16)
    vb = v_ref[0].astype(BF16)
    for h in range(MEM_HEADS):
        sl = slice(h * MEM_HEADDIM, (h + 1) * MEM_HEADDIM)
        q = q_ref[0, :, sl]
        ms = jnp.mean(q * q, axis=-1, keepdims=True)
        qn = (q * lax.rsqrt(ms + EPS) * qw_ref[...]).astype(BF16)
        s = _dot_nt(qn, kb[:, sl]) * (MEM_HEADDIM ** -0.5)
        e = jnp.exp(s - jnp.max(s, axis=-1, keepdims=True))
        p = e / jnp.sum(e, axis=-1, keepdims=True)
        o_ref[0, :, sl] = _dot(p.astype(BF16), vb[:, sl]).astype(BF16)


def _mem_attend(q3, qcol, mk, mv, qw, tq):
    nb, s, _ = q3.shape
    m = mk.shape[1]
    return pl.pallas_call(
        _mem_kernel,
        grid=(nb, s // tq),
        in_specs=[pl.BlockSpec((1, tq, D_MODEL), lambda b, i: (b, i, qcol)),
                  pl.BlockSpec((1, m, D_MODEL), lambda b, i: (b, 0, 0)),
                  pl.BlockSpec((1, m, D_MODEL), lambda b, i: (b, 0, 0)),
                  pl.BlockSpec((1, MEM_HEADDIM), lambda b, i: (0, 0))],
        out_specs=pl.BlockSpec((1, tq, D_MODEL), lambda b, i: (b, i, 0)),
        out_shape=jax.ShapeDtypeStruct((nb, s, D_MODEL), BF16),
        compiler_params=_params(("parallel", "parallel")),
        name="mem_attend",
    )(q3, mk, mv, qw)


def _merge_kernel(ys_ref, oa_ref, om_ref, g1_ref, g2_ref, g3_ref, x_ref, ws_ref, wa_ref, wm_ref, wo_ref, nw_ref,
                  x1_ref, hf_ref):
    merged = (_sigmoid(g1_ref[...]) * _dot(ys_ref[...], ws_ref[...])
              + _sigmoid(g2_ref[...]) * _dot(oa_ref[...], wa_ref[...])
              + _sigmoid(g3_ref[...]) * _dot(om_ref[...], wm_ref[...]))
    x1 = x_ref[...] + _dot(merged.astype(BF16), wo_ref[...])
    x1_ref[...] = x1
    ms = jnp.mean(x1 * x1, axis=-1, keepdims=True)
    hf_ref[...] = (x1 * lax.rsqrt(ms + EPS) * nw_ref[...]).astype(BF16)


def _merge(y_ssd, o_att, o_mem, p2, x2, ws, wa, wm, wo, nw, tm):
    t = x2.shape[0]
    tok = lambda w, c=0: pl.BlockSpec((tm, w), lambda i: (i, c))
    res = lambda a: pl.BlockSpec(a.shape, lambda i: (0, 0))
    return pl.pallas_call(
        _merge_kernel,
        grid=(t // tm,),
        in_specs=[tok(SSD_INNER), tok(D_MODEL), tok(D_MODEL),
                  tok(D_MODEL, COL_G), tok(D_MODEL, COL_G + 1), tok(D_MODEL, COL_G + 2), tok(D_MODEL),
                  res(ws), res(wa), res(wm), res(wo), res(nw)],
        out_specs=[tok(D_MODEL), tok(D_MODEL)],
        out_shape=[jax.ShapeDtypeStruct((t, D_MODEL), F32), jax.ShapeDtypeStruct((t, D_MODEL), BF16)],
        compiler_params=_params(("parallel",)),
        name="merge",
    )(y_ssd, o_att, o_mem, p2, p2, p2, x2, ws, wa, wm, wo, nw)


def _top16(src_scr, n_rows, tp):
    r_id = _iota((n_rows, tp), 0)
    slot = _iota((PEER_TOPK, tp), 0)

    def body(r, carry):
        vals, idxs = carry
        sc = src_scr[...]
        m = jnp.max(sc, axis=0, keepdims=True)
        idx = jnp.min(jnp.where(sc == m, r_id, n_rows), axis=0, keepdims=True)
        src_scr[...] = jnp.where(r_id == idx, NEG_INF, sc)
        return jnp.where(slot == r, m, vals), jnp.where(slot == r, idx, idxs)

    return lax.fori_loop(0, PEER_TOPK, body, (jnp.zeros((PEER_TOPK, tp), F32), jnp.zeros((PEER_TOPK, tp), I32)))


def _peer_sel_kernel(hf_ref, wqt_ref, keys_ref, a_ref, b_ref, g_ref,
                     qt_scr, sc_scr, sv_scr, si_scr, cand_scr, oa_scr, ob_scr, og_scr):
    tp = hf_ref.shape[0]
    nk = PEER_NKEYS
    kk = PEER_TOPK
    qt_scr[...] = _dot_nt(wqt_ref[...], hf_ref[...])

    def half(hc, carry):
        q_hc = qt_scr[pl.ds(pl.multiple_of(hc * nk, nk), nk), :]
        sc_scr[...] = _dot3(keys_ref[hc], q_hc, _dot)
        vals, idxs = _top16(sc_scr, nk, tp)
        sv_scr[hc] = vals
        si_scr[hc] = idxs
        return carry

    lax.fori_loop(0, 2 * PEER_HEADS, half, 0)

    def head(h, carry):
        sv1 = sv_scr[2 * h]
        sv2 = sv_scr[2 * h + 1]
        si1 = si_scr[2 * h]
        si2 = si_scr[2 * h + 1]
        for a in range(kk):
            cand_scr[a * kk:(a + 1) * kk, :] = sv1[a:a + 1, :] + sv2
        fv, fi = _top16(cand_scr, kk * kk, tp)
        fa = fi // kk
        fb = fi - fa * kk
        e1 = jnp.zeros((kk, tp), I32)
        e2 = jnp.zeros((kk, tp), I32)
        for a in range(kk):
            e1 = e1 + jnp.where(fa == a, si1[a:a + 1, :], 0)
            e2 = e2 + jnp.where(fb == a, si2[a:a + 1, :], 0)
        e = jnp.exp(fv - fv[0:1, :])
        rows = pl.ds(pl.multiple_of(h * kk, kk), kk)
        oa_scr[rows, :] = e1.astype(F32)
        ob_scr[rows, :] = e2.astype(F32)
        og_scr[rows, :] = e / jnp.sum(e, axis=0, keepdims=True)
        return carry

    lax.fori_loop(0, PEER_HEADS, head, 0)
    a_ref[...] = oa_scr[...].T
    b_ref[...] = ob_scr[...].T
    g_ref[...] = og_scr[...].T


def _peer_sel(hf, wqt, keys, tp):
    t = hf.shape[0]
    kk = PEER_TOPK
    npick = PEER_HEADS * kk
    out = pl.BlockSpec((tp, npick), lambda i: (i, 0))
    return pl.pallas_call(
        _peer_sel_kernel,
        grid=(t // tp,),
        in_specs=[pl.BlockSpec((tp, D_MODEL), lambda i: (i, 0)),
                  pl.BlockSpec(wqt.shape, lambda i: (0, 0)),
                  pl.BlockSpec(keys.shape, lambda i: (0, 0, 0))],
        out_specs=[out, out, out],
        out_shape=[jax.ShapeDtypeStruct((t, npick), F32)] * 3,
        scratch_shapes=[pltpu.VMEM((wqt.shape[0], tp), F32),
                        pltpu.VMEM((PEER_NKEYS, tp), F32),
                        pltpu.VMEM((2 * PEER_HEADS, kk, tp), F32),
                        pltpu.VMEM((2 * PEER_HEADS, kk, tp), I32),
                        pltpu.VMEM((kk * kk, tp), F32),
                        pltpu.VMEM((npick, tp), F32),
                        pltpu.VMEM((npick, tp), F32),
                        pltpu.VMEM((npick, tp), F32)],
        compiler_params=_params(("parallel",)),
        name="peer_sel",
    )(hf, wqt, keys)


def _peer_kernel(hf_ref, a_ref, b_ref, g_ref, x1_ref, u_ref, v_ref, o_ref, g_scr, *, tm, te):
    nk = PEER_NKEYS
    e = pl.program_id(1)
    n1 = te // nk

    @pl.when(e == 0)
    def _():
        o_ref[...] = x1_ref[...]
        sub = _iota((nk, LANES), 0).astype(F32)

        def tok(t, carry):
            row = pl.ds(t, 1)
            y = jnp.where(sub == a_ref[row, :], g_ref[row, :], 0.0).astype(BF16)
            z = jnp.where(sub == b_ref[row, :], 1.0, 0.0).astype(BF16)
            g_scr[pl.ds(pl.multiple_of(t * nk, nk), nk), :] = _dot_nt(y, z)
            return carry

        lax.fori_loop(0, tm, tok, 0)

    act = _dot_nt(hf_ref[...], u_ref[...])
    act = 0.5 * act * (1.0 + lax.erf(act * (2.0 ** -0.5)))
    ws = []
    for k in range(n1):
        gk = g_scr[pl.ds(e * n1 + k, tm, stride=nk), :]
        ws.append((act[:, k * nk:(k + 1) * nk] * gk).astype(BF16))
    o_ref[...] += _dot(jnp.concatenate(ws, axis=1), v_ref[...])


def _peer(hf, a, b, g, x1, u, v, tm, te):
    t = hf.shape[0]
    n_exp = u.shape[0]
    npick = a.shape[1]
    tok = lambda w: pl.BlockSpec((tm, w), lambda i, e: (i, 0))
    exp = pl.BlockSpec((te, D_MODEL), lambda i, e: (e, 0))
    return pl.pallas_call(
        functools.partial(_peer_kernel, tm=tm, te=te),
        grid=(t // tm, n_exp // te),
        in_specs=[tok(D_MODEL), tok(npick), tok(npick), tok(npick), tok(D_MODEL), exp, exp],
        out_specs=tok(D_MODEL),
        out_shape=jax.ShapeDtypeStruct((t, D_MODEL), F32),
        scratch_shapes=[pltpu.VMEM((tm * PEER_NKEYS, PEER_NKEYS), F32)],
        compiler_params=_params(("parallel", "arbitrary")),
        name="peer",
    )(hf, a, b, g, x1, u, v)


def _pack_w_in(w_in):
    n0 = SSD_INNER + SSD_CONV_DIM
    w_main = jnp.concatenate([w_in[:, :n0], w_in[:, n0 + SSD_HEADS:]], axis=1).astype(BF16)
    w_dt = jnp.pad(w_in[:, n0:n0 + SSD_HEADS], ((0, 0), (0, LANES - SSD_HEADS))).astype(BF16)
    return w_main, w_dt


def _pad_lanes(v):
    return jnp.pad(v, (0, LANES - v.shape[0]))[None]


def kernel(x_prompt, x_sample, mem_prompt, cache_k, cache_v, page_table, cache_mem_k, cache_mem_v, state_ssm, state_conv, norm1_w, w_in, conv_w, conv_b, dt_bias, a_log, d_skip, ssd_norm_w, q_norm_w, k_norm_w, mem_norm_w, w_mem_k, w_mem_v, mem_qn_w, mem_kn_w, w_br_ssd, w_br_attn, w_br_mem, w_out, norm2_w, peer_wq, peer_keys, peer_u, peer_v):
    bp, sp, _ = x_prompt.shape
    bs, ss, _ = x_sample.shape
    assert ss == 1 and cache_k.shape[0] == 1
    tp, ts = bp * sp, bs * ss

    w_main, w_dt = _pack_w_in(w_in[0])
    nw1 = norm1_w[0][None]
    ssd_w = (conv_w[0], conv_b[0][None], _pad_lanes(dt_bias[0]), _pad_lanes(a_log[0]),
             jnp.repeat(d_skip[0], SSD_HEADDIM)[None], ssd_norm_w[0][None])
    qw2 = jnp.tile(q_norm_w[0], LANES // ATT_HEADDIM)[None]
    kw2 = jnp.tile(k_norm_w[0], LANES // ATT_HEADDIM)[None]
    w_mem_kv = jnp.concatenate([w_mem_k[0], w_mem_v[0]], axis=1).astype(BF16)
    merge_w = (w_br_ssd[0].astype(BF16), w_br_attn[0].astype(BF16), w_br_mem[0].astype(BF16),
               w_out[0].astype(BF16), norm2_w[0][None])
    wqt = peer_wq[0].T.astype(BF16)
    keys = peer_keys[0].reshape(2 * PEER_HEADS, PEER_NKEYS, -1)
    u_bf = peer_u[0].astype(BF16)
    v_bf = peer_v[0].astype(BF16)
    mem_qw = mem_qn_w[0][None]

    xp = x_prompt.reshape(tp, D_MODEL)
    p_main = _proj(xp, nw1, w_main, 1024, 1024)
    p_dt = _proj(xp, nw1, w_dt, 1024, LANES)
    p3 = p_main.reshape(bp, sp, N_MAIN)
    y_ssd, ssm_p = _ssd(p3, p_dt.reshape(bp, sp, LANES), jnp.zeros((bp, SUBLANES, SSD_CONV_DIM), F32),
                        jnp.zeros((bp, SSD_HEADS, SSD_HEADDIM, SSD_STATE), F32), *ssd_w, SSD_CHUNK)
    cos, sin = _rope_tables(jnp.arange(sp, dtype=I32))
    qa, k_rot, ka, vt, sel = _qk(p3, cos, sin, qw2, kw2)
    o_att = _attn(qa, ka, vt, sel)
    n_mem = mem_prompt.shape[1]
    kv = _proj(mem_prompt.reshape(bp * n_mem, D_MODEL), mem_norm_w[0][None], w_mem_kv, 256, D_MODEL)
    mem_k = _headnorm(kv, mem_kn_w[0][None], 256)
    mem_v = kv[:, D_MODEL:]
    o_mem = _mem_attend(p3, COL_MQ, mem_k.reshape(bp, n_mem, D_MODEL), mem_v.reshape(bp, n_mem, D_MODEL), mem_qw, 512)
    x1, hf = _merge(y_ssd.reshape(tp, SSD_INNER), o_att.reshape(tp, D_MODEL), o_mem.reshape(tp, D_MODEL),
                    p_main, xp, *merge_w, 512)
    pa, pb, pg = _peer_sel(hf, wqt, keys, 256)
    y_p = _peer(hf, pa, pb, pg, x1, u_bf, v_bf, 256, 512)

    xs = x_sample.reshape(ts, D_MODEL)
    ps_main = _proj(xs, nw1, w_main, ts, 1024)
    ps_dt = _proj(xs, nw1, w_dt, ts, LANES)
    ps3 = ps_main.reshape(bs, 1, N_MAIN)
    conv_prev8 = jnp.pad(state_conv[0], ((0, 0), (SUBLANES - (SSD_CONV - 1), 0), (0, 0)))
    ys_ssd, ssm_s = _ssd(ps3, ps_dt.reshape(bs, 1, LANES), conv_prev8, state_ssm[0], *ssd_w, 1)
    cos_s, sin_s = _rope_tables(jnp.full((ts,), PAST_LEN, I32))
    q_rot_s, k_rot_s = _qk_s(ps_main, cos_s, sin_s, qw2, kw2)
    v_s = ps_main[:, COL_V * D_MODEL:(COL_V + 1) * D_MODEL]
    n_pool = cache_k.shape[1]
    o_att_s = _dec_attn(q_rot_s.reshape(bs, 1, D_MODEL), k_rot_s.reshape(bs, 1, D_MODEL), v_s.reshape(bs, 1, D_MODEL),
                        cache_k[0].reshape(n_pool, PAGE_SIZE, D_MODEL), cache_v[0].reshape(n_pool, PAGE_SIZE, D_MODEL),
                        page_table)
    mq8 = jnp.broadcast_to(ps3[:, :, COL_MQ * D_MODEL:(COL_MQ + 1) * D_MODEL], (bs, SUBLANES, D_MODEL))
    o_mem_s = _mem_attend(mq8, 0, cache_mem_k[0].reshape(bs, n_mem, D_MODEL),
                          cache_mem_v[0].reshape(bs, n_mem, D_MODEL), mem_qw, SUBLANES)[:, 0]
    x1_s, hf_s = _merge(ys_ssd.reshape(ts, SSD_INNER), o_att_s.reshape(ts, D_MODEL).astype(BF16), o_mem_s,
                        ps_main, xs, *merge_w, ts)
    pad_rows = lambda v: jnp.pad(v, ((0, LANES - ts), (0, 0)))
    hf_sp = pad_rows(hf_s)
    sa, sb, sg = _peer_sel(hf_sp, wqt, keys, LANES)
    y_s = _peer(hf_sp, sa, sb, sg, pad_rows(x1_s), u_bf, v_bf, LANES, 512)[:ts]

    kv_shape = (1, bp, sp, ATT_HEADS, ATT_HEADDIM)
    mem_shape = (1, bp, n_mem, MEM_HEADS, MEM_HEADDIM)
    xbc_cols = slice(COL_X * 1024, COL_X * 1024 + SSD_CONV_DIM)
    conv_p = p3[:, sp - (SSD_CONV - 1):, xbc_cols]
    conv_s = jnp.concatenate([state_conv[0][:, 1:], ps3[:, :, xbc_cols]], axis=1)
    return (y_p.reshape(bp, sp, D_MODEL), y_s.reshape(bs, ss, D_MODEL),
            k_rot.reshape(kv_shape), p3[:, :, COL_V * 1024:(COL_V + 1) * 1024].reshape(kv_shape),
            mem_k.reshape(mem_shape), mem_v.reshape(mem_shape),
            ssm_p[None], conv_p[None],
            k_rot_s.reshape(1, bs, ss, ATT_HEADS, ATT_HEADDIM), v_s.reshape(1, bs, ss, ATT_HEADS, ATT_HEADDIM),
            ssm_s[None], conv_s[None])
```

```python
import functools
import math

import jax
import jax.numpy as jnp
from jax import lax
from jax.experimental import pallas as pl
from jax.experimental.pallas import tpu as pltpu

F32 = jnp.float32
BF16 = jnp.bfloat16
I32 = jnp.int32
EPS = 1e-6
NEG_INF = float("-inf")

D_MODEL = 1024
SSD_INNER = 2048
SSD_HEADS = 32
SSD_HEADDIM = 64
SSD_GROUPS = 8
SSD_STATE = 128
SSD_CONV = 4
SSD_CHUNK = 128
SSD_CONV_DIM = 4096
ATT_HEADS = 16
ATT_HEADDIM = 64
MOBA_BLOCK = 256
MOBA_TOPK = 3
ROPE_THETA = 10000.0
MEM_HEADS = 4
MEM_HEADDIM = 256
PEER_HEADS = 8
PEER_NKEYS = 128
PEER_TOPK = 16
PEER_NCAND = 80
PAST_LEN = 16384
PAGE_SIZE = 128

LANES = 128
SUBLANES = 8
VMEM_LIMIT = 56 * 1024 * 1024

COL_Z, COL_X, COL_BC, COL_Q, COL_K, COL_V, COL_MQ, COL_G = 0, 2, 4, 6, 7, 8, 9, 10
N_MAIN = 13 * 1024


def _params(sem):
    return pltpu.CompilerParams(dimension_semantics=sem, vmem_limit_bytes=VMEM_LIMIT)


def _dot(a, b):
    return jnp.dot(a, b, preferred_element_type=F32)


def _dot_nt(a, b):
    return lax.dot_general(a, b, (((1,), (1,)), ((), ())), preferred_element_type=F32)


def _dot_tn(a, b):
    return lax.dot_general(a, b, (((0,), (0,)), ((), ())), preferred_element_type=F32)


def _split(a):
    hi = a.astype(BF16)
    lo = (a - hi.astype(F32)).astype(BF16)
    return hi, lo


def _dot3(a, b, dot):
    ah, al = _split(a)
    bh, bl = _split(b)
    return dot(ah, bh) + (dot(ah, bl) + dot(al, bh))


def _sigmoid(x):
    return 1.0 / (1.0 + jnp.exp(-x))


def _iota(shape, axis):
    return lax.broadcasted_iota(I32, shape, axis)


def _tree(x, op):
    while x.shape[0] > 1:
        h = x.shape[0] // 2
        x = op(x[:h], x[h:])
    return x[0]


def _proj_kernel(x_ref, nw_ref, w_ref, o_ref, h_scr):
    @pl.when(pl.program_id(1) == 0)
    def _():
        x = x_ref[...]
        ms = jnp.mean(x * x, axis=-1, keepdims=True)
        h_scr[...] = (x * lax.rsqrt(ms + EPS) * nw_ref[...]).astype(BF16)

    o_ref[...] = _dot(h_scr[...], w_ref[...])


def _proj(x, nw, w, tm, tn):
    t, d = x.shape
    n = w.shape[1]
    return pl.pallas_call(
        _proj_kernel,
        grid=(t // tm, n // tn),
        in_specs=[pl.BlockSpec((tm, d), lambda i, j: (i, 0)),
                  pl.BlockSpec((1, d), lambda i, j: (0, 0)),
                  pl.BlockSpec((d, tn), lambda i, j: (0, j))],
        out_specs=pl.BlockSpec((tm, tn), lambda i, j: (i, j)),
        out_shape=jax.ShapeDtypeStruct((t, n), F32),
        scratch_shapes=[pltpu.VMEM((tm, d), BF16)],
        compiler_params=_params(("parallel", "arbitrary")),
        name="proj",
    )(x, nw, w)


def _ssd_kernel(z_ref, x_ref, bc_ref, dt_ref, cprev_ref, h0_ref, cw_ref, cb_ref, dtb_ref, alog_ref,
                dsk_ref, nw_ref, y_ref, hl_ref, buf, st, ycat, *, rows):
    L = SSD_CHUNK
    c = pl.program_id(1)

    @pl.when(c == 0)
    def _():
        buf[0:SUBLANES, :] = cprev_ref[0]
        st[...] = h0_ref[0]

    if rows < L:
        buf[SUBLANES:, :] = jnp.zeros((L, SSD_CONV_DIM), F32)
    buf[SUBLANES:SUBLANES + rows, 0:SSD_INNER] = x_ref[0]
    buf[SUBLANES:SUBLANES + rows, SSD_INNER:] = bc_ref[0]

    acc = buf[SUBLANES:SUBLANES + L, :] * cw_ref[SSD_CONV - 1:SSD_CONV, :]
    for j in range(SSD_CONV - 1):
        off = SUBLANES - (SSD_CONV - 1) + j
        acc = acc + buf[off:off + L, :] * cw_ref[j:j + 1, :]
    acc = acc + cb_ref[...]
    xbc = acc * _sigmoid(acc)
    buf[0:SUBLANES, :] = buf[L:L + SUBLANES, :]

    xs = xbc[:, :SSD_INNER]
    n_gn = SSD_GROUPS * SSD_STATE
    bm = xbc[:, SSD_INNER:SSD_INNER + n_gn].astype(BF16)
    cm = xbc[:, SSD_INNER + n_gn:].astype(BF16)

    dt_raw = dt_ref[0] if rows == L else jnp.broadcast_to(dt_ref[0], (L, LANES))
    pre = dt_raw + dtb_ref[...]
    dt = jnp.maximum(pre, 0.0) + jnp.log1p(jnp.exp(-jnp.abs(pre)))
    row_id = _iota((L, LANES), 0)
    dt = jnp.where(row_id < rows, dt, 0.0)
    a = -jnp.exp(alog_ref[...])
    da = dt * a
    tri = _iota((L, L), 0) >= _iota((L, L), 1)
    a_cs = jnp.dot(tri.astype(F32), da, preferred_element_type=F32, precision=lax.Precision.HIGHEST)
    a_cs_t = a_cs.T
    a_last = a_cs[L - 1:L, :]
    e_cs = jnp.exp(a_cs)
    d_end = jnp.exp(a_last - a_cs)
    e_last = jnp.exp(a_last)

    for g in range(SSD_GROUPS):
        bg = bm[:, g * SSD_STATE:(g + 1) * SSD_STATE]
        cg = cm[:, g * SSD_STATE:(g + 1) * SSD_STATE]
        cb = _dot_nt(cg, bg)
        for r in range(SSD_HEADS // SSD_GROUPS):
            h = g * (SSD_HEADS // SSD_GROUPS) + r
            hs = slice(h * SSD_HEADDIM, (h + 1) * SSD_HEADDIM)
            col = a_cs[:, h:h + 1]
            row = a_cs_t[h:h + 1, :]
            decay = jnp.where(tri, jnp.exp(col - row), 0.0)
            m = (cb * decay).astype(BF16)
            xh = xs[:, hs]
            xdt = xh * dt[:, h:h + 1]
            yd = _dot(m, xdt.astype(BF16))
            hp = st[h]
            yo = _dot_nt(cg, hp.astype(BF16)) * e_cs[:, h:h + 1]
            xw = (xdt * d_end[:, h:h + 1]).astype(BF16)
            st[h] = hp * e_last[:, h:h + 1] + _dot_tn(xw, bg)
            ycat[:, hs] = yd + yo + xh * dsk_ref[:, hs]

    z = z_ref[0]
    if rows < L:
        y = ycat[0:rows, :] * (z * _sigmoid(z))
    else:
        y = ycat[...] * (z * _sigmoid(z))
    ms = jnp.mean(y * y, axis=-1, keepdims=True)
    y_ref[0] = (y * lax.rsqrt(ms + EPS) * nw_ref[...]).astype(BF16)

    @pl.when(c == pl.num_programs(1) - 1)
    def _():
        hl_ref[0] = st[...]


def _ssd(p3, dt3, conv_prev8, h0, cw, cb, dtb, alog, dsk, nw, rows):
    nb, s, _ = p3.shape
    nc = s // rows
    w2 = SSD_INNER
    const = lambda shape: pl.BlockSpec(shape, lambda b, c: tuple(0 for _ in shape))
    return pl.pallas_call(
        functools.partial(_ssd_kernel, rows=rows),
        grid=(nb, nc),
        in_specs=[pl.BlockSpec((1, rows, w2), lambda b, c: (b, c, COL_Z // 2)),
                  pl.BlockSpec((1, rows, w2), lambda b, c: (b, c, COL_X // 2)),
                  pl.BlockSpec((1, rows, w2), lambda b, c: (b, c, COL_BC // 2)),
                  pl.BlockSpec((1, rows, LANES), lambda b, c: (b, c, 0)),
                  pl.BlockSpec((1, SUBLANES, SSD_CONV_DIM), lambda b, c: (b, 0, 0)),
                  pl.BlockSpec((1, SSD_HEADS, SSD_HEADDIM, SSD_STATE), lambda b, c: (b, 0, 0, 0)),
                  const((SSD_CONV, SSD_CONV_DIM)), const((1, SSD_CONV_DIM)), const((1, LANES)),
                  const((1, LANES)), const((1, SSD_INNER)), const((1, SSD_INNER))],
        out_specs=[pl.BlockSpec((1, rows, w2), lambda b, c: (b, c, 0)),
                   pl.BlockSpec((1, SSD_HEADS, SSD_HEADDIM, SSD_STATE), lambda b, c: (b, 0, 0, 0))],
        out_shape=[jax.ShapeDtypeStruct((nb, s, w2), BF16),
                   jax.ShapeDtypeStruct((nb, SSD_HEADS, SSD_HEADDIM, SSD_STATE), F32)],
        scratch_shapes=[pltpu.VMEM((SUBLANES + SSD_CHUNK, SSD_CONV_DIM), F32),
                        pltpu.VMEM((SSD_HEADS, SSD_HEADDIM, SSD_STATE), F32),
                        pltpu.VMEM((SSD_CHUNK, SSD_INNER), F32)],
        compiler_params=_params(("parallel", "arbitrary")),
        name="ssd",
    )(p3, p3, p3, dt3, conv_prev8, h0, cw, cb, dtb, alog, dsk, nw)


def _rope_tables(pos):
    d = ATT_HEADDIM
    inv = 1.0 / (ROPE_THETA ** (jnp.arange(0, d, 2, dtype=F32) / d))
    ang = pos.astype(F32)[:, None] * inv[None, :]
    cos = jnp.cos(ang)
    sin = jnp.sin(ang)
    cos = jnp.concatenate([cos, cos, cos, cos], axis=-1)
    sin = jnp.concatenate([-sin, sin, -sin, sin], axis=-1)
    return cos, sin


def _head_norm_rope(x, w, cos, sin, bd, first):
    hi, lo = _split(x * x)
    ms = _dot(hi, bd) + _dot(lo, bd)
    xn = x * lax.rsqrt(ms + EPS) * w
    partner = jnp.where(first, pltpu.roll(xn, LANES - ATT_HEADDIM // 2, 1), pltpu.roll(xn, ATT_HEADDIM // 2, 1))
    return xn * cos + partner * sin


def _head_consts(rows):
    r = _iota((LANES, LANES), 0) // ATT_HEADDIM
    c = _iota((LANES, LANES), 1) // ATT_HEADDIM
    bd = jnp.where(r == c, 1.0 / ATT_HEADDIM, 0.0).astype(BF16)
    first = (_iota((rows, LANES), 1) % ATT_HEADDIM) < (ATT_HEADDIM // 2)
    return bd, first


def _qk_kernel(q_ref, k_ref, v_ref, cos_ref, sin_ref, qw_ref, kw_ref,
               qa_ref, kr_ref, ka_ref, vt_ref, sel_ref, km_scr, qr_scr, *, nblk):
    tb = MOBA_BLOCK
    i = pl.program_id(1)

    @pl.when(i == 0)
    def _():
        km_scr[...] = jnp.zeros_like(km_scr)

    cos = cos_ref[...]
    sin = sin_ref[...]
    bd, first = _head_consts(tb)
    for cb in range(D_MODEL // LANES):
        sl = slice(cb * LANES, (cb + 1) * LANES)
        qr = _head_norm_rope(q_ref[0, :, sl], qw_ref[...], cos, sin, bd, first)
        kr = _head_norm_rope(k_ref[0, :, sl], kw_ref[...], cos, sin, bd, first)
        qr_scr[:, sl] = qr
        qa_ref[0, :, sl] = (qr * (ATT_HEADDIM ** -0.5)).astype(BF16)
        kr_ref[0, :, sl] = kr
        ka_ref[0, :, sl] = kr.astype(BF16)
        k_mean = jnp.sum(kr, axis=0, keepdims=True) * (1.0 / tb)
        km_scr[:, sl] = jnp.where(_iota((nblk, LANES), 0) == i, k_mean, km_scr[:, sl])
    vt_ref[0, 0] = v_ref[0].T.astype(BF16)

    km = km_scr[...]
    lane_head = _iota((nblk, D_MODEL), 1) // ATT_HEADDIM
    kmt = jnp.concatenate([jnp.where(lane_head == h, km, 0.0) for h in range(ATT_HEADS)], axis=0)
    gate = _dot3(kmt, qr_scr[...], _dot_nt)
    n_id = _iota((nblk, tb), 0)
    valid = n_id < i
    for h in range(ATT_HEADS):
        g = jnp.where(valid, gate[h * nblk:(h + 1) * nblk, :], NEG_INF)
        chosen = jnp.zeros((nblk, tb), F32)
        for _ in range(MOBA_TOPK):
            m = jnp.max(g, axis=0, keepdims=True)
            idx = jnp.min(jnp.where(g == m, n_id, nblk), axis=0, keepdims=True)
            pick = n_id == idx
            chosen = jnp.where(pick & valid, 1.0, chosen)
            g = jnp.where(pick, NEG_INF, g)
        sel_ref[0, h] = chosen


def _qk(p3, cos, sin, qw, kw):
    nb, s, _ = p3.shape
    tb = MOBA_BLOCK
    nblk = s // tb
    col = lambda cidx: pl.BlockSpec((1, tb, D_MODEL), lambda b, i: (b, i, cidx))
    out_tok = pl.BlockSpec((1, tb, D_MODEL), lambda b, i: (b, i, 0))
    return pl.pallas_call(
        functools.partial(_qk_kernel, nblk=nblk),
        grid=(nb, nblk),
        in_specs=[col(COL_Q), col(COL_K), col(COL_V),
                  pl.BlockSpec((tb, LANES), lambda b, i: (i, 0)),
                  pl.BlockSpec((tb, LANES), lambda b, i: (i, 0)),
                  pl.BlockSpec((1, LANES), lambda b, i: (0, 0)),
                  pl.BlockSpec((1, LANES), lambda b, i: (0, 0))],
        out_specs=[out_tok, out_tok, out_tok,
                   pl.BlockSpec((1, 1, D_MODEL, tb), lambda b, i: (b, i, 0, 0)),
                   pl.BlockSpec((1, ATT_HEADS, nblk, tb), lambda b, i: (b, 0, 0, i))],
        out_shape=[jax.ShapeDtypeStruct((nb, s, D_MODEL), BF16),
                   jax.ShapeDtypeStruct((nb, s, D_MODEL), F32),
                   jax.ShapeDtypeStruct((nb, s, D_MODEL), BF16),
                   jax.ShapeDtypeStruct((nb, nblk, D_MODEL, tb), BF16),
                   jax.ShapeDtypeStruct((nb, ATT_HEADS, nblk, s), F32)],
        scratch_shapes=[pltpu.VMEM((nblk, D_MODEL), F32), pltpu.VMEM((tb, D_MODEL), F32)],
        compiler_params=_params(("parallel", "arbitrary")),
        name="qk",
    )(p3, p3, p3, cos, sin, qw, kw)


def _attn_kernel(q_ref, k_ref, vt_ref, sel_ref, o_ref):
    tb = MOBA_BLOCK
    hd = ATT_HEADDIM
    i = pl.program_id(2)
    nh = LANES // hd
    q = q_ref[0]
    lane_head = _iota((tb, LANES), 1) // hd
    causal = _iota((tb, tb), 0) <= _iota((tb, tb), 1)
    qhs = [jnp.where(lane_head == hh, q, jnp.zeros_like(q)) for hh in range(nh)]

    kd = k_ref[0, pl.ds(pl.multiple_of(i * tb, tb), tb), :]
    state = []
    for hh in range(nh):
        s = jnp.where(causal, _dot_nt(kd, qhs[hh]), NEG_INF)
        m = jnp.max(s, axis=0, keepdims=True)
        p = jnp.exp(s - m)
        state += [m, jnp.sum(p, axis=0, keepdims=True),
                  _dot(vt_ref[0, i, hh * hd:(hh + 1) * hd, :], p.astype(BF16))]

    first_blk = _iota((2 * tb, tb), 0) < tb

    def body(jj, carry):
        j = 2 * jj
        kj = k_ref[0, pl.ds(pl.multiple_of(j * tb, 2 * tb), 2 * tb), :]
        ss = [_dot_nt(kj, qhs[hh]) for hh in range(nh)]
        new = []
        for hh in range(nh):
            m, l, acc = carry[3 * hh:3 * hh + 3]
            on = jnp.where(first_blk, sel_ref[0, hh, pl.ds(j, 1), :], sel_ref[0, hh, pl.ds(j + 1, 1), :]) > 0.5
            s = jnp.where(on, ss[hh], NEG_INF)
            m_new = jnp.maximum(m, jnp.max(s, axis=0, keepdims=True))
            p = jnp.exp(s - m_new)
            alpha = jnp.exp(m - m_new)
            vj = jnp.concatenate([vt_ref[0, j, hh * hd:(hh + 1) * hd, :],
                                  vt_ref[0, j + 1, hh * hd:(hh + 1) * hd, :]], axis=1)
            new += [m_new, alpha * l + jnp.sum(p, axis=0, keepdims=True), acc * alpha + _dot(vj, p.astype(BF16))]
        return tuple(new)

    state = lax.fori_loop(0, (i + 1) // 2, body, tuple(state))
    outs = [state[3 * hh + 2] / state[3 * hh + 1] for hh in range(nh)]
    o_ref[0] = jnp.concatenate(outs, axis=0).T.astype(BF16)


def _attn(qa, ka, vt, sel):
    nb, s, _ = qa.shape
    tb = MOBA_BLOCK
    nblk = s // tb
    npair = D_MODEL // LANES
    hp = LANES // ATT_HEADDIM
    return pl.pallas_call(
        _attn_kernel,
        grid=(nb, npair, nblk),
        in_specs=[pl.BlockSpec((1, tb, LANES), lambda b, p, i: (b, i, p)),
                  pl.BlockSpec((1, s, LANES), lambda b, p, i: (b, 0, p)),
                  pl.BlockSpec((1, nblk, LANES, tb), lambda b, p, i: (b, 0, p, 0)),
                  pl.BlockSpec((1, hp, nblk, tb), lambda b, p, i: (b, p, 0, i))],
        out_specs=pl.BlockSpec((1, tb, LANES), lambda b, p, i: (b, i, p)),
        out_shape=jax.ShapeDtypeStruct((nb, s, D_MODEL), BF16),
        compiler_params=_params(("parallel", "parallel", "arbitrary")),
        name="attn",
    )(qa, ka, vt, sel)


def _qk_s_kernel(q_ref, k_ref, cos_ref, sin_ref, qw_ref, kw_ref, qr_ref, kr_ref):
    rows = q_ref.shape[0]
    cos = cos_ref[...]
    sin = sin_ref[...]
    bd, first = _head_consts(rows)
    for cb in range(D_MODEL // LANES):
        sl = slice(cb * LANES, (cb + 1) * LANES)
        qr_ref[:, sl] = _head_norm_rope(q_ref[:, sl], qw_ref[...], cos, sin, bd, first)
        kr_ref[:, sl] = _head_norm_rope(k_ref[:, sl], kw_ref[...], cos, sin, bd, first)


def _qk_s(p2, cos, sin, qw, kw):
    n = p2.shape[0]
    col = lambda cidx: pl.BlockSpec((n, D_MODEL), lambda i: (0, cidx))
    full = lambda r: pl.BlockSpec((r, LANES), lambda i: (0, 0))
    return pl.pallas_call(
        _qk_s_kernel,
        grid=(1,),
        in_specs=[col(COL_Q), col(COL_K), full(n), full(n), full(1), full(1)],
        out_specs=[pl.BlockSpec((n, D_MODEL), lambda i: (0, 0))] * 2,
        out_shape=[jax.ShapeDtypeStruct((n, D_MODEL), F32)] * 2,
        compiler_params=_params(("arbitrary",)),
        name="qk_s",
    )(p2, p2, cos, sin, qw, kw)


def _dec_gate_kernel(pt_ref, q_ref, k0_ref, k1_ref, sel_ref, gate_scr, *, nblk):
    del pt_ref
    j = pl.program_id(1)
    lane = _iota((ATT_HEADS, LANES), 1)

    @pl.when(j == 0)
    def _():
        gate_scr[...] = jnp.full_like(gate_scr, NEG_INF)

    km = _tree(k0_ref[...] + k1_ref[...], jnp.add) * (1.0 / MOBA_BLOCK)
    g = jnp.sum(km * q_ref[0], axis=-1, keepdims=True)
    gate_scr[...] = jnp.where(lane == j, g, gate_scr[...])

    @pl.when(j == nblk - 1)
    def _():
        g = gate_scr[...]
        chosen = jnp.zeros((ATT_HEADS, LANES), F32)
        for _ in range(MOBA_TOPK):
            m = jnp.max(g, axis=-1, keepdims=True)
            idx = jnp.min(jnp.where(g == m, lane, LANES), axis=-1, keepdims=True)
            pick = lane == idx
            chosen = jnp.where(pick & (lane < nblk), 1.0, chosen)
            g = jnp.where(pick, NEG_INF, g)
        sel_ref[0] = chosen


def _dec_attend_kernel(pt_ref, fetch_ref, flag_ref, q_ref, kn_ref, vn_ref, sel_ref, k0_ref, k1_ref, v0_ref, v1_ref,
                       o_ref, m_scr, l_scr, acc_scr):
    del pt_ref, fetch_ref
    b = pl.program_id(0)
    j = pl.program_id(1)
    hd = ATT_HEADDIM
    qa = q_ref[0] * (hd ** -0.5)
    ones = jnp.ones((hd, LANES), BF16)

    def lane_sum(x):
        hi, lo = _split(x)
        return _dot(hi, ones) + _dot(lo, ones)

    @pl.when(j == 0)
    def _():
        m_scr[...] = lane_sum(qa * kn_ref[0])
        l_scr[...] = jnp.ones_like(l_scr)
        acc_scr[...] = vn_ref[0]

    @pl.when(flag_ref[b, j] == 1)
    def _():
        lane = _iota((ATT_HEADS, LANES), 1)
        on = jnp.sum(jnp.where(lane == j, sel_ref[0], 0.0), axis=-1, keepdims=True) > 0.5
        kb = jnp.concatenate([k0_ref[...], k1_ref[...]], axis=0)
        s = lane_sum((kb * qa).reshape(MOBA_BLOCK * ATT_HEADS, hd)).reshape(MOBA_BLOCK, ATT_HEADS, LANES)
        s = jnp.where(on, s, NEG_INF)
        m = m_scr[...]
        m_new = jnp.maximum(m, _tree(s, jnp.maximum))
        p = jnp.exp(s - m_new)
        alpha = jnp.exp(m - m_new)
        vb = jnp.concatenate([v0_ref[...], v1_ref[...]], axis=0)
        m_scr[...] = m_new
        l_scr[...] = alpha * l_scr[...] + _tree(p, jnp.add)
        acc_scr[...] = alpha[:, :hd] * acc_scr[...] + _tree(p[:, :, :hd] * vb, jnp.add)

    @pl.when(j == pl.num_programs(1) - 1)
    def _():
        o_ref[0] = acc_scr[...] / l_scr[:, :hd]


def _dec_attn(q_rot, k_new, v_new, pool_k, pool_v, page_table):
    nb = q_rot.shape[0]
    n_pages = page_table.shape[1]
    per = MOBA_BLOCK // PAGE_SIZE
    nblk = n_pages // per
    assert MOBA_TOPK <= nblk <= LANES
    hd_shape = (1, ATT_HEADS, ATT_HEADDIM)
    page_shape = (None, PAGE_SIZE, ATT_HEADS, ATT_HEADDIM)

    gate_spec = pltpu.PrefetchScalarGridSpec(
        num_scalar_prefetch=1,
        grid=(nb, nblk),
        in_specs=[pl.BlockSpec(hd_shape, lambda b, j, pt: (b, 0, 0)),
                  pl.BlockSpec(page_shape, lambda b, j, pt: (pt[b, per * j], 0, 0, 0)),
                  pl.BlockSpec(page_shape, lambda b, j, pt: (pt[b, per * j + 1], 0, 0, 0))],
        out_specs=pl.BlockSpec((1, ATT_HEADS, LANES), lambda b, j, pt: (b, 0, 0)),
        scratch_shapes=[pltpu.VMEM((ATT_HEADS, LANES), F32)])
    sel = pl.pallas_call(
        functools.partial(_dec_gate_kernel, nblk=nblk),
        grid_spec=gate_spec,
        out_shape=jax.ShapeDtypeStruct((nb, ATT_HEADS, LANES), F32),
        compiler_params=_params(("parallel", "arbitrary")),
        name="dec_gate",
    )(page_table, q_rot, pool_k, pool_k)

    blk = jnp.arange(nblk, dtype=I32)
    flag = (jnp.max(sel[:, :, :nblk], axis=1) > 0.5).astype(I32)
    fetch = lax.cummax(jnp.where(flag == 1, blk[None, :], 0), axis=1)

    tok = pl.BlockSpec(hd_shape, lambda b, j, pt, fe, fl: (b, 0, 0))

    def page(half):
        return pl.BlockSpec(page_shape, lambda b, j, pt, fe, fl: (pt[b, per * fe[b, j] + half], 0, 0, 0))

    attend_spec = pltpu.PrefetchScalarGridSpec(
        num_scalar_prefetch=3,
        grid=(nb, nblk),
        in_specs=[tok, tok, tok, pl.BlockSpec((1, ATT_HEADS, LANES), lambda b, j, pt, fe, fl: (b, 0, 0)),
                  page(0), page(1), page(0), page(1)],
        out_specs=tok,
        scratch_shapes=[pltpu.VMEM((ATT_HEADS, LANES), F32), pltpu.VMEM((ATT_HEADS, LANES), F32),
                        pltpu.VMEM((ATT_HEADS, ATT_HEADDIM), F32)])
    return pl.pallas_call(
        _dec_attend_kernel,
        grid_spec=attend_spec,
        out_shape=jax.ShapeDtypeStruct((nb, ATT_HEADS, ATT_HEADDIM), F32),
        compiler_params=_params(("parallel", "arbitrary")),
        name="dec_attend",
    )(page_table, fetch, flag, q_rot, k_new, v_new, sel, pool_k, pool_k, pool_v, pool_v)


def _headnorm_kernel(x_ref, w_ref, o_ref):
    for h in range(MEM_HEADS):
        sl = slice(h * MEM_HEADDIM, (h + 1) * MEM_HEADDIM)
        x = x_ref[:, sl]
        ms = jnp.mean(x * x, axis=-1, keepdims=True)
        o_ref[:, sl] = x * lax.rsqrt(ms + EPS) * w_ref[...]


def _headnorm(x2, w, tm):
    t = x2.shape[0]
    return pl.pallas_call(
        _headnorm_kernel,
        grid=(t // tm,),
        in_specs=[pl.BlockSpec((tm, D_MODEL), lambda i: (i, 0)), pl.BlockSpec((1, MEM_HEADDIM), lambda i: (0, 0))],
        out_specs=pl.BlockSpec((tm, D_MODEL), lambda i: (i, 0)),
        out_shape=jax.ShapeDtypeStruct((t, D_MODEL), F32),
        compiler_params=_params(("parallel",)),
        name="headnorm",
    )(x2, w)


def _mem_kernel(q_ref, k_ref, v_ref, qw_ref, o_ref):
    kb = k_ref[0].astype(BF16)
    vb = v_ref[0].astype(BF16)
    for h in range(MEM_HEADS):
        sl = slice(h * MEM_HEADDIM, (h + 1) * MEM_HEADDIM)
        q = q_ref[0, :, sl]
        ms = jnp.mean(q * q, axis=-1, keepdims=True)
        qn = (q * lax.rsqrt(ms + EPS) * qw_ref[...]).astype(BF16)
        s = _dot_nt(qn, kb[:, sl]) * (MEM_HEADDIM ** -0.5)
        e = jnp.exp(s - jnp.max(s, axis=-1, keepdims=True))
        p = e / jnp.sum(e, axis=-1, keepdims=True)
        o_ref[0, :, sl] = _dot(p.astype(BF16), vb[:, sl]).astype(BF16)


def _mem_attend(q3, qcol, mk, mv, qw, tq):
    nb, s, _ = q3.shape
    m = mk.shape[1]
    return pl.pallas_call(
        _mem_kernel,
        grid=(nb, s // tq),
        in_specs=[pl.BlockSpec((1, tq, D_MODEL), lambda b, i: (b, i, qcol)),
                  pl.BlockSpec((1, m, D_MODEL), lambda b, i: (b, 0, 0)),
                  pl.BlockSpec((1, m, D_MODEL), lambda b, i: (b, 0, 0)),
                  pl.BlockSpec((1, MEM_HEADDIM), lambda b, i: (0, 0))],
        out_specs=pl.BlockSpec((1, tq, D_MODEL), lambda b, i: (b, i, 0)),
        out_shape=jax.ShapeDtypeStruct((nb, s, D_MODEL), BF16),
        compiler_params=_params(("parallel", "parallel")),
        name="mem_attend",
    )(q3, mk, mv, qw)


def _merge_kernel(ys_ref, oa_ref, om_ref, g1_ref, g2_ref, g3_ref, x_ref, ws_ref, wa_ref, wm_ref, wo_ref, nw_ref,
                  x1_ref, hf_ref):
    merged = (_sigmoid(g1_ref[...]) * _dot(ys_ref[...], ws_ref[...])
              + _sigmoid(g2_ref[...]) * _dot(oa_ref[...], wa_ref[...])
              + _sigmoid(g3_ref[...]) * _dot(om_ref[...], wm_ref[...]))
    x1 = x_ref[...] + _dot(merged.astype(BF16), wo_ref[...])
    x1_ref[...] = x1
    ms = jnp.mean(x1 * x1, axis=-1, keepdims=True)
    hf_ref[...] = (x1 * lax.rsqrt(ms + EPS) * nw_ref[...]).astype(BF16)


def _merge(y_ssd, o_att, o_mem, p2, x2, ws, wa, wm, wo, nw, tm):
    t = x2.shape[0]
    tok = lambda w, c=0: pl.BlockSpec((tm, w), lambda i: (i, c))
    res = lambda a: pl.BlockSpec(a.shape, lambda i: (0, 0))
    return pl.pallas_call(
        _merge_kernel,
        grid=(t // tm,),
        in_specs=[tok(SSD_INNER), tok(D_MODEL), tok(D_MODEL),
                  tok(D_MODEL, COL_G), tok(D_MODEL, COL_G + 1), tok(D_MODEL, COL_G + 2), tok(D_MODEL),
                  res(ws), res(wa), res(wm), res(wo), res(nw)],
        out_specs=[tok(D_MODEL), tok(D_MODEL)],
        out_shape=[jax.ShapeDtypeStruct((t, D_MODEL), F32), jax.ShapeDtypeStruct((t, D_MODEL), BF16)],
        compiler_params=_params(("parallel",)),
        name="merge",
    )(y_ssd, o_att, o_mem, p2, p2, p2, x2, ws, wa, wm, wo, nw)


def _top16(src_scr, n_rows, tp):
    r_id = _iota((n_rows, tp), 0)
    slot = _iota((PEER_TOPK, tp), 0)

    def body(r, carry):
        vals, idxs = carry
        sc = src_scr[...]
        m = jnp.max(sc, axis=0, keepdims=True)
        idx = jnp.min(jnp.where(sc == m, r_id, n_rows), axis=0, keepdims=True)
        src_scr[...] = jnp.where(r_id == idx, NEG_INF, sc)
        return jnp.where(slot == r, m, vals), jnp.where(slot == r, idx, idxs)

    return lax.fori_loop(0, PEER_TOPK, body, (jnp.zeros((PEER_TOPK, tp), F32), jnp.zeros((PEER_TOPK, tp), I32)))


def _peer_sel_kernel(hf_ref, wqt_ref, keys_ref, a_ref, b_ref, g_ref,
                     qt_scr, sc_scr, sv_scr, si_scr, cand_scr, oa_scr, ob_scr, og_scr):
    tp = hf_ref.shape[0]
    nk = PEER_NKEYS
    kk = PEER_TOPK
    qt_scr[...] = _dot_nt(wqt_ref[...], hf_ref[...])

    def half(hc, carry):
        q_hc = qt_scr[pl.ds(pl.multiple_of(hc * nk, nk), nk), :]
        sc_scr[...] = _dot3(keys_ref[hc], q_hc, _dot)
        vals, idxs = _top16(sc_scr, nk, tp)
        sv_scr[hc] = vals
        si_scr[hc] = idxs
        return carry

    lax.fori_loop(0, 2 * PEER_HEADS, half, 0)

    def head(h, carry):
        sv1 = sv_scr[2 * h]
        sv2 = sv_scr[2 * h + 1]
        si1 = si_scr[2 * h]
        si2 = si_scr[2 * h + 1]
        cand_scr[0:kk, :] = sv1[0:1, :] + sv2
        b8 = _iota((SUBLANES, tp), 0)
        for a in range(1, SUBLANES):
            c = sv1[a:a + 1, :] + sv2[0:SUBLANES, :]
            n_b = kk // (a + 1)
            cand_scr[SUBLANES * (a + 1):SUBLANES * (a + 2), :] = c if n_b >= SUBLANES else jnp.where(b8 < n_b, c, NEG_INF)
        cand_scr[PEER_NCAND - SUBLANES:, :] = sv1[SUBLANES:, :] + sv2[0:1, :]
        fv, fr = _top16(cand_scr, PEER_NCAND, tp)
        mid = fr - kk
        fa = jnp.where(fr < kk, 0, jnp.where(fr < PEER_NCAND - SUBLANES, 1 + (mid >> 3), fr - (PEER_NCAND - 2 * SUBLANES)))
        fb = jnp.where(fr < kk, fr, jnp.where(fr < PEER_NCAND - SUBLANES, mid & (SUBLANES - 1), 0))
        e1 = jnp.zeros((kk, tp), I32)
        e2 = jnp.zeros((kk, tp), I32)
        for a in range(kk):
            e1 = e1 + jnp.where(fa == a, si1[a:a + 1, :], 0)
            e2 = e2 + jnp.where(fb == a, si2[a:a + 1, :], 0)
        e = jnp.exp(fv - fv[0:1, :])
        rows = pl.ds(pl.multiple_of(h * kk, kk), kk)
        oa_scr[rows, :] = e1.astype(F32)
        ob_scr[rows, :] = e2.astype(F32)
        og_scr[rows, :] = e / jnp.sum(e, axis=0, keepdims=True)
        return carry

    lax.fori_loop(0, PEER_HEADS, head, 0)
    a_ref[...] = oa_scr[...].T
    b_ref[...] = ob_scr[...].T
    g_ref[...] = og_scr[...].T


def _peer_sel(hf, wqt, keys, tp):
    t = hf.shape[0]
    kk = PEER_TOPK
    npick = PEER_HEADS * kk
    out = pl.BlockSpec((tp, npick), lambda i: (i, 0))
    return pl.pallas_call(
        _peer_sel_kernel,
        grid=(t // tp,),
        in_specs=[pl.BlockSpec((tp, D_MODEL), lambda i: (i, 0)),
                  pl.BlockSpec(wqt.shape, lambda i: (0, 0)),
                  pl.BlockSpec(keys.shape, lambda i: (0, 0, 0))],
        out_specs=[out, out, out],
        out_shape=[jax.ShapeDtypeStruct((t, npick), F32)] * 3,
        scratch_shapes=[pltpu.VMEM((wqt.shape[0], tp), F32),
                        pltpu.VMEM((PEER_NKEYS, tp), F32),
                        pltpu.VMEM((2 * PEER_HEADS, kk, tp), F32),
                        pltpu.VMEM((2 * PEER_HEADS, kk, tp), I32),
                        pltpu.VMEM((PEER_NCAND, tp), F32),
                        pltpu.VMEM((npick, tp), F32),
                        pltpu.VMEM((npick, tp), F32),
                        pltpu.VMEM((npick, tp), F32)],
        compiler_params=_params(("parallel",)),
        name="peer_sel",
    )(hf, wqt, keys)


def _peer_kernel(hf_ref, a_ref, b_ref, g_ref, x1_ref, u_ref, v_ref, o_ref, g_scr, *, tm, te):
    nk = PEER_NKEYS
    e = pl.program_id(1)
    n1 = te // nk

    @pl.when(e == 0)
    def _():
        o_ref[...] = x1_ref[...]
        sub = _iota((nk, LANES), 0).astype(F32)

        def toks(t8, carry):
            for u in range(SUBLANES):
                t = t8 * SUBLANES + u
                row = pl.ds(t, 1)
                y = jnp.where(sub == a_ref[row, :], g_ref[row, :], 0.0).astype(BF16)
                z = jnp.where(sub == b_ref[row, :], 1.0, 0.0).astype(BF16)
                g_scr[pl.ds(pl.multiple_of(t * nk, nk), nk), :] = _dot_nt(y, z)
            return carry

        lax.fori_loop(0, tm // SUBLANES, toks, 0)

    hf = hf_ref[...]
    half = te // 2
    for c in range(2):
        act = _dot_nt(hf, u_ref[c * half:(c + 1) * half, :])
        act = 0.5 * act * (1.0 + lax.erf(act * (2.0 ** -0.5)))
        ws = []
        for k in range(half // nk):
            gk = g_scr[pl.ds(e * n1 + c * (half // nk) + k, tm, stride=nk), :]
            ws.append((act[:, k * nk:(k + 1) * nk] * gk).astype(BF16))
        o_ref[...] += _dot(jnp.concatenate(ws, axis=1), v_ref[c * half:(c + 1) * half, :])


def _peer(hf, a, b, g, x1, u, v, tm, te):
    t = hf.shape[0]
    n_exp = u.shape[0]
    npick = a.shape[1]
    tok = lambda w: pl.BlockSpec((tm, w), lambda i, e: (i, 0))
    exp = pl.BlockSpec((te, D_MODEL), lambda i, e: (e, 0))
    return pl.pallas_call(
        functools.partial(_peer_kernel, tm=tm, te=te),
        grid=(t // tm, n_exp // te),
        in_specs=[tok(D_MODEL), tok(npick), tok(npick), tok(npick), tok(D_MODEL), exp, exp],
        out_specs=tok(D_MODEL),
        out_shape=jax.ShapeDtypeStruct((t, D_MODEL), F32),
        scratch_shapes=[pltpu.VMEM((tm * PEER_NKEYS, PEER_NKEYS), F32)],
        compiler_params=_params(("parallel", "arbitrary")),
        name="peer",
    )(hf, a, b, g, x1, u, v)


def _pack_w_in(w_in):
    n0 = SSD_INNER + SSD_CONV_DIM
    w_main = jnp.concatenate([w_in[:, :n0], w_in[:, n0 + SSD_HEADS:]], axis=1).astype(BF16)
    w_dt = jnp.pad(w_in[:, n0:n0 + SSD_HEADS], ((0, 0), (0, LANES - SSD_HEADS))).astype(BF16)
    return w_main, w_dt


def _pad_lanes(v):
    return jnp.pad(v, (0, LANES - v.shape[0]))[None]


def kernel(x_prompt, x_sample, mem_prompt, cache_k, cache_v, page_table, cache_mem_k, cache_mem_v, state_ssm, state_conv, norm1_w, w_in, conv_w, conv_b, dt_bias, a_log, d_skip, ssd_norm_w, q_norm_w, k_norm_w, mem_norm_w, w_mem_k, w_mem_v, mem_qn_w, mem_kn_w, w_br_ssd, w_br_attn, w_br_mem, w_out, norm2_w, peer_wq, peer_keys, peer_u, peer_v):
    bp, sp, _ = x_prompt.shape
    bs, ss, _ = x_sample.shape
    assert ss == 1 and cache_k.shape[0] == 1
    tp, ts = bp * sp, bs * ss

    w_main, w_dt = _pack_w_in(w_in[0])
    nw1 = norm1_w[0][None]
    ssd_w = (conv_w[0], conv_b[0][None], _pad_lanes(dt_bias[0]), _pad_lanes(a_log[0]),
             jnp.repeat(d_skip[0], SSD_HEADDIM)[None], ssd_norm_w[0][None])
    qw2 = jnp.tile(q_norm_w[0], LANES // ATT_HEADDIM)[None]
    kw2 = jnp.tile(k_norm_w[0], LANES // ATT_HEADDIM)[None]
    w_mem_kv = jnp.concatenate([w_mem_k[0], w_mem_v[0]], axis=1).astype(BF16)
    merge_w = (w_br_ssd[0].astype(BF16), w_br_attn[0].astype(BF16), w_br_mem[0].astype(BF16),
               w_out[0].astype(BF16), norm2_w[0][None])
    wqt = peer_wq[0].T.astype(BF16)
    keys = peer_keys[0].reshape(2 * PEER_HEADS, PEER_NKEYS, -1)
    u_bf = peer_u[0].astype(BF16)
    v_bf = peer_v[0].astype(BF16)
    mem_qw = mem_qn_w[0][None]

    xp = x_prompt.reshape(tp, D_MODEL)
    p_main = _proj(xp, nw1, w_main, 1024, 1024)
    p_dt = _proj(xp, nw1, w_dt, 1024, LANES)
    p3 = p_main.reshape(bp, sp, N_MAIN)
    y_ssd, ssm_p = _ssd(p3, p_dt.reshape(bp, sp, LANES), jnp.zeros((bp, SUBLANES, SSD_CONV_DIM), F32),
                        jnp.zeros((bp, SSD_HEADS, SSD_HEADDIM, SSD_STATE), F32), *ssd_w, SSD_CHUNK)
    cos, sin = _rope_tables(jnp.arange(sp, dtype=I32))
    qa, k_rot, ka, vt, sel = _qk(p3, cos, sin, qw2, kw2)
    o_att = _attn(qa, ka, vt, sel)
    n_mem = mem_prompt.shape[1]
    kv = _proj(mem_prompt.reshape(bp * n_mem, D_MODEL), mem_norm_w[0][None], w_mem_kv, 256, D_MODEL)
    mem_k = _headnorm(kv, mem_kn_w[0][None], 256)
    mem_v = kv[:, D_MODEL:]
    o_mem = _mem_attend(p3, COL_MQ, mem_k.reshape(bp, n_mem, D_MODEL), mem_v.reshape(bp, n_mem, D_MODEL), mem_qw, 512)
    x1, hf = _merge(y_ssd.reshape(tp, SSD_INNER), o_att.reshape(tp, D_MODEL), o_mem.reshape(tp, D_MODEL),
                    p_main, xp, *merge_w, 512)
    pa, pb, pg = _peer_sel(hf, wqt, keys, 256)
    y_p = _peer(hf, pa, pb, pg, x1, u_bf, v_bf, 256, 1024)

    xs = x_sample.reshape(ts, D_MODEL)
    ps_main = _proj(xs, nw1, w_main, ts, 1024)
    ps_dt = _proj(xs, nw1, w_dt, ts, LANES)
    ps3 = ps_main.reshape(bs, 1, N_MAIN)
    conv_prev8 = jnp.pad(state_conv[0], ((0, 0), (SUBLANES - (SSD_CONV - 1), 0), (0, 0)))
    ys_ssd, ssm_s = _ssd(ps3, ps_dt.reshape(bs, 1, LANES), conv_prev8, state_ssm[0], *ssd_w, 1)
    cos_s, sin_s = _rope_tables(jnp.full((ts,), PAST_LEN, I32))
    q_rot_s, k_rot_s = _qk_s(ps_main, cos_s, sin_s, qw2, kw2)
    v_s = ps_main[:, COL_V * D_MODEL:(COL_V + 1) * D_MODEL]
    heads = lambda v: v.reshape(bs, ATT_HEADS, ATT_HEADDIM)
    o_att_s = _dec_attn(heads(q_rot_s), heads(k_rot_s), heads(v_s), cache_k[0], cache_v[0], page_table)
    mq8 = jnp.broadcast_to(ps3[:, :, COL_MQ * D_MODEL:(COL_MQ + 1) * D_MODEL], (bs, SUBLANES, D_MODEL))
    o_mem_s = _mem_attend(mq8, 0, cache_mem_k[0].reshape(bs, n_mem, D_MODEL),
                          cache_mem_v[0].reshape(bs, n_mem, D_MODEL), mem_qw, SUBLANES)[:, 0]
    x1_s, hf_s = _merge(ys_ssd.reshape(ts, SSD_INNER), o_att_s.reshape(ts, D_MODEL).astype(BF16), o_mem_s,
                        ps_main, xs, *merge_w, ts)
    pad_rows = lambda v: jnp.pad(v, ((0, LANES - ts), (0, 0)))
    hf_sp = pad_rows(hf_s)
    sa, sb, sg = _peer_sel(hf_sp, wqt, keys, LANES)
    y_s = _peer(hf_sp, sa, sb, sg, pad_rows(x1_s), u_bf, v_bf, LANES, 1024)[:ts]

    kv_shape = (1, bp, sp, ATT_HEADS, ATT_HEADDIM)
    mem_shape = (1, bp, n_mem, MEM_HEADS, MEM_HEADDIM)
    xbc_cols = slice(COL_X * 1024, COL_X * 1024 + SSD_CONV_DIM)
    conv_p = p3[:, sp - (SSD_CONV - 1):, xbc_cols]
    conv_s = jnp.concatenate([state_conv[0][:, 1:], ps3[:, :, xbc_cols]], axis=1)
    return (y_p.reshape(bp, sp, D_MODEL), y_s.reshape(bs, ss, D_MODEL),
            k_rot.reshape(kv_shape), p3[:, :, COL_V * 1024:(COL_V + 1) * 1024].reshape(kv_shape),
            mem_k.reshape(mem_shape), mem_v.reshape(mem_shape),
            ssm_p[None], conv_p[None],
            k_rot_s.reshape(1, bs, ss, ATT_HEADS, ATT_HEADDIM), v_s.reshape(1, bs, ss, ATT_HEADS, ATT_HEADDIM),
            ssm_s[None], conv_s[None])
```

```python
import functools
import math

import jax
import jax.numpy as jnp
from jax import lax
from jax.experimental import pallas as pl
from jax.experimental.pallas import tpu as pltpu

F32 = jnp.float32
BF16 = jnp.bfloat16
I32 = jnp.int32
EPS = 1e-6
NEG_INF = float("-inf")

D_MODEL = 1024
SSD_INNER = 2048
SSD_HEADS = 32
SSD_HEADDIM = 64
SSD_GROUPS = 8
SSD_STATE = 128
SSD_CONV = 4
SSD_CHUNK = 128
SSD_CONV_DIM = 4096
ATT_HEADS = 16
ATT_HEADDIM = 64
MOBA_BLOCK = 256
MOBA_TOPK = 3
ROPE_THETA = 10000.0
MEM_HEADS = 4
MEM_HEADDIM = 256
PEER_HEADS = 8
PEER_NKEYS = 128
PEER_TOPK = 16
PEER_NCAND = 80
PAST_LEN = 16384
PAGE_SIZE = 128

LANES = 128
SUBLANES = 8
VMEM_LIMIT = 56 * 1024 * 1024

COL_Z, COL_X, COL_BC, COL_Q, COL_K, COL_V, COL_MQ, COL_G = 0, 2, 4, 6, 7, 8, 9, 10
N_MAIN = 13 * 1024


def _params(sem):
    return pltpu.CompilerParams(dimension_semantics=sem, vmem_limit_bytes=VMEM_LIMIT)


def _dot(a, b):
    return jnp.dot(a, b, preferred_element_type=F32)


def _dot_nt(a, b):
    return lax.dot_general(a, b, (((1,), (1,)), ((), ())), preferred_element_type=F32)


def _dot_tn(a, b):
    return lax.dot_general(a, b, (((0,), (0,)), ((), ())), preferred_element_type=F32)


def _split(a):
    hi = a.astype(BF16)
    lo = (a - hi.astype(F32)).astype(BF16)
    return hi, lo


def _dot3(a, b, dot):
    ah, al = _split(a)
    bh, bl = _split(b)
    return dot(ah, bh) + (dot(ah, bl) + dot(al, bh))


def _sigmoid(x):
    return 1.0 / (1.0 + jnp.exp(-x))


def _iota(shape, axis):
    return lax.broadcasted_iota(I32, shape, axis)


def _tree(x, op):
    while x.shape[0] > 1:
        h = x.shape[0] // 2
        x = op(x[:h], x[h:])
    return x[0]


def _proj_kernel(x_ref, nw_ref, w_ref, o_ref, h_scr):
    @pl.when(pl.program_id(1) == 0)
    def _():
        x = x_ref[...]
        ms = jnp.mean(x * x, axis=-1, keepdims=True)
        h_scr[...] = (x * lax.rsqrt(ms + EPS) * nw_ref[...]).astype(BF16)

    o_ref[...] = _dot(h_scr[...], w_ref[...])


def _proj(x, nw, w, tm, tn):
    t, d = x.shape
    n = w.shape[1]
    return pl.pallas_call(
        _proj_kernel,
        grid=(t // tm, n // tn),
        in_specs=[pl.BlockSpec((tm, d), lambda i, j: (i, 0)),
                  pl.BlockSpec((1, d), lambda i, j: (0, 0)),
                  pl.BlockSpec((d, tn), lambda i, j: (0, j))],
        out_specs=pl.BlockSpec((tm, tn), lambda i, j: (i, j)),
        out_shape=jax.ShapeDtypeStruct((t, n), F32),
        scratch_shapes=[pltpu.VMEM((tm, d), BF16)],
        compiler_params=_params(("parallel", "arbitrary")),
        name="proj",
    )(x, nw, w)


def _ssd_kernel(z_ref, x_ref, bc_ref, dt_ref, cprev_ref, h0_ref, cw_ref, cb_ref, dtb_ref, alog_ref,
                dsk_ref, nw_ref, y_ref, hl_ref, buf, st, ycat, *, rows):
    L = SSD_CHUNK
    c = pl.program_id(1)

    @pl.when(c == 0)
    def _():
        buf[0:SUBLANES, :] = cprev_ref[0]
        st[...] = h0_ref[0]

    if rows < L:
        buf[SUBLANES:, :] = jnp.zeros((L, SSD_CONV_DIM), F32)
    buf[SUBLANES:SUBLANES + rows, 0:SSD_INNER] = x_ref[0]
    buf[SUBLANES:SUBLANES + rows, SSD_INNER:] = bc_ref[0]

    acc = buf[SUBLANES:SUBLANES + L, :] * cw_ref[SSD_CONV - 1:SSD_CONV, :]
    for j in range(SSD_CONV - 1):
        off = SUBLANES - (SSD_CONV - 1) + j
        acc = acc + buf[off:off + L, :] * cw_ref[j:j + 1, :]
    acc = acc + cb_ref[...]
    xbc = acc * _sigmoid(acc)
    buf[0:SUBLANES, :] = buf[L:L + SUBLANES, :]

    xs = xbc[:, :SSD_INNER]
    n_gn = SSD_GROUPS * SSD_STATE
    bm = xbc[:, SSD_INNER:SSD_INNER + n_gn].astype(BF16)
    cm = xbc[:, SSD_INNER + n_gn:].astype(BF16)

    dt_raw = dt_ref[0] if rows == L else jnp.broadcast_to(dt_ref[0], (L, LANES))
    pre = dt_raw + dtb_ref[...]
    dt = jnp.maximum(pre, 0.0) + jnp.log1p(jnp.exp(-jnp.abs(pre)))
    row_id = _iota((L, LANES), 0)
    dt = jnp.where(row_id < rows, dt, 0.0)
    a = -jnp.exp(alog_ref[...])
    da = dt * a
    tri = _iota((L, L), 0) >= _iota((L, L), 1)
    a_cs = jnp.dot(tri.astype(F32), da, preferred_element_type=F32, precision=lax.Precision.HIGHEST)
    a_cs_t = a_cs.T
    a_last = a_cs[L - 1:L, :]
    e_cs = jnp.exp(a_cs)
    d_end = jnp.exp(a_last - a_cs)
    e_last = jnp.exp(a_last)

    for g in range(SSD_GROUPS):
        bg = bm[:, g * SSD_STATE:(g + 1) * SSD_STATE]
        cg = cm[:, g * SSD_STATE:(g + 1) * SSD_STATE]
        cb = _dot_nt(cg, bg)
        for r in range(SSD_HEADS // SSD_GROUPS):
            h = g * (SSD_HEADS // SSD_GROUPS) + r
            hs = slice(h * SSD_HEADDIM, (h + 1) * SSD_HEADDIM)
            col = a_cs[:, h:h + 1]
            row = a_cs_t[h:h + 1, :]
            decay = jnp.where(tri, jnp.exp(col - row), 0.0)
            m = (cb * decay).astype(BF16)
            xh = xs[:, hs]
            xdt = xh * dt[:, h:h + 1]
            yd = _dot(m, xdt.astype(BF16))
            hp = st[h]
            yo = _dot_nt(cg, hp.astype(BF16)) * e_cs[:, h:h + 1]
            xw = (xdt * d_end[:, h:h + 1]).astype(BF16)
            st[h] = hp * e_last[:, h:h + 1] + _dot_tn(xw, bg)
            ycat[:, hs] = yd + yo + xh * dsk_ref[:, hs]

    z = z_ref[0]
    if rows < L:
        y = ycat[0:rows, :] * (z * _sigmoid(z))
    else:
        y = ycat[...] * (z * _sigmoid(z))
    ms = jnp.mean(y * y, axis=-1, keepdims=True)
    y_ref[0] = (y * lax.rsqrt(ms + EPS) * nw_ref[...]).astype(BF16)

    @pl.when(c == pl.num_programs(1) - 1)
    def _():
        hl_ref[0] = st[...]


def _ssd(p3, dt3, conv_prev8, h0, cw, cb, dtb, alog, dsk, nw, rows):
    nb, s, _ = p3.shape
    nc = s // rows
    w2 = SSD_INNER
    const = lambda shape: pl.BlockSpec(shape, lambda b, c: tuple(0 for _ in shape))
    return pl.pallas_call(
        functools.partial(_ssd_kernel, rows=rows),
        grid=(nb, nc),
        in_specs=[pl.BlockSpec((1, rows, w2), lambda b, c: (b, c, COL_Z // 2)),
                  pl.BlockSpec((1, rows, w2), lambda b, c: (b, c, COL_X // 2)),
                  pl.BlockSpec((1, rows, w2), lambda b, c: (b, c, COL_BC // 2)),
                  pl.BlockSpec((1, rows, LANES), lambda b, c: (b, c, 0)),
                  pl.BlockSpec((1, SUBLANES, SSD_CONV_DIM), lambda b, c: (b, 0, 0)),
                  pl.BlockSpec((1, SSD_HEADS, SSD_HEADDIM, SSD_STATE), lambda b, c: (b, 0, 0, 0)),
                  const((SSD_CONV, SSD_CONV_DIM)), const((1, SSD_CONV_DIM)), const((1, LANES)),
                  const((1, LANES)), const((1, SSD_INNER)), const((1, SSD_INNER))],
        out_specs=[pl.BlockSpec((1, rows, w2), lambda b, c: (b, c, 0)),
                   pl.BlockSpec((1, SSD_HEADS, SSD_HEADDIM, SSD_STATE), lambda b, c: (b, 0, 0, 0))],
        out_shape=[jax.ShapeDtypeStruct((nb, s, w2), BF16),
                   jax.ShapeDtypeStruct((nb, SSD_HEADS, SSD_HEADDIM, SSD_STATE), F32)],
        scratch_shapes=[pltpu.VMEM((SUBLANES + SSD_CHUNK, SSD_CONV_DIM), F32),
                        pltpu.VMEM((SSD_HEADS, SSD_HEADDIM, SSD_STATE), F32),
                        pltpu.VMEM((SSD_CHUNK, SSD_INNER), F32)],
        compiler_params=_params(("parallel", "arbitrary")),
        name="ssd",
    )(p3, p3, p3, dt3, conv_prev8, h0, cw, cb, dtb, alog, dsk, nw)


def _rope_tables(pos):
    d = ATT_HEADDIM
    inv = 1.0 / (ROPE_THETA ** (jnp.arange(0, d, 2, dtype=F32) / d))
    ang = pos.astype(F32)[:, None] * inv[None, :]
    cos = jnp.cos(ang)
    sin = jnp.sin(ang)
    cos = jnp.concatenate([cos, cos, cos, cos], axis=-1)
    sin = jnp.concatenate([-sin, sin, -sin, sin], axis=-1)
    return cos, sin


def _head_norm_rope(x, w, cos, sin, bd, first):
    hi, lo = _split(x * x)
    ms = _dot(hi, bd) + _dot(lo, bd)
    xn = x * lax.rsqrt(ms + EPS) * w
    partner = jnp.where(first, pltpu.roll(xn, LANES - ATT_HEADDIM // 2, 1), pltpu.roll(xn, ATT_HEADDIM // 2, 1))
    return xn * cos + partner * sin


def _head_consts(rows):
    r = _iota((LANES, LANES), 0) // ATT_HEADDIM
    c = _iota((LANES, LANES), 1) // ATT_HEADDIM
    bd = jnp.where(r == c, 1.0 / ATT_HEADDIM, 0.0).astype(BF16)
    first = (_iota((rows, LANES), 1) % ATT_HEADDIM) < (ATT_HEADDIM // 2)
    return bd, first


def _qk_kernel(q_ref, k_ref, v_ref, cos_ref, sin_ref, qw_ref, kw_ref,
               qa_ref, kr_ref, ka_ref, vt_ref, sel_ref, km_scr, qr_scr, *, nblk):
    tb = MOBA_BLOCK
    i = pl.program_id(1)

    @pl.when(i == 0)
    def _():
        km_scr[...] = jnp.zeros_like(km_scr)

    cos = cos_ref[...]
    sin = sin_ref[...]
    bd, first = _head_consts(tb)
    for cb in range(D_MODEL // LANES):
        sl = slice(cb * LANES, (cb + 1) * LANES)
        qr = _head_norm_rope(q_ref[0, :, sl], qw_ref[...], cos, sin, bd, first)
        kr = _head_norm_rope(k_ref[0, :, sl], kw_ref[...], cos, sin, bd, first)
        qr_scr[:, sl] = qr
        qa_ref[0, :, sl] = (qr * (ATT_HEADDIM ** -0.5)).astype(BF16)
        kr_ref[0, :, sl] = kr
        ka_ref[0, :, sl] = kr.astype(BF16)
        k_mean = jnp.sum(kr, axis=0, keepdims=True) * (1.0 / tb)
        km_scr[:, sl] = jnp.where(_iota((nblk, LANES), 0) == i, k_mean, km_scr[:, sl])
    vt_ref[0, 0] = v_ref[0].T.astype(BF16)

    km = km_scr[...]
    lane_head = _iota((nblk, D_MODEL), 1) // ATT_HEADDIM
    kmt = jnp.concatenate([jnp.where(lane_head == h, km, 0.0) for h in range(ATT_HEADS)], axis=0)
    gate = _dot3(kmt, qr_scr[...], _dot_nt)
    n_id = _iota((nblk, tb), 0)
    valid = n_id < i
    for h in range(ATT_HEADS):
        g = jnp.where(valid, gate[h * nblk:(h + 1) * nblk, :], NEG_INF)
        chosen = jnp.zeros((nblk, tb), F32)
        for _ in range(MOBA_TOPK):
            m = jnp.max(g, axis=0, keepdims=True)
            idx = jnp.min(jnp.where(g == m, n_id, nblk), axis=0, keepdims=True)
            pick = n_id == idx
            chosen = jnp.where(pick & valid, 1.0, chosen)
            g = jnp.where(pick, NEG_INF, g)
        sel_ref[0, h] = chosen


def _qk(p3, cos, sin, qw, kw):
    nb, s, _ = p3.shape
    tb = MOBA_BLOCK
    nblk = s // tb
    col = lambda cidx: pl.BlockSpec((1, tb, D_MODEL), lambda b, i: (b, i, cidx))
    out_tok = pl.BlockSpec((1, tb, D_MODEL), lambda b, i: (b, i, 0))
    return pl.pallas_call(
        functools.partial(_qk_kernel, nblk=nblk),
        grid=(nb, nblk),
        in_specs=[col(COL_Q), col(COL_K), col(COL_V),
                  pl.BlockSpec((tb, LANES), lambda b, i: (i, 0)),
                  pl.BlockSpec((tb, LANES), lambda b, i: (i, 0)),
                  pl.BlockSpec((1, LANES), lambda b, i: (0, 0)),
                  pl.BlockSpec((1, LANES), lambda b, i: (0, 0))],
        out_specs=[out_tok, out_tok, out_tok,
                   pl.BlockSpec((1, 1, D_MODEL, tb), lambda b, i: (b, i, 0, 0)),
                   pl.BlockSpec((1, ATT_HEADS, nblk, tb), lambda b, i: (b, 0, 0, i))],
        out_shape=[jax.ShapeDtypeStruct((nb, s, D_MODEL), BF16),
                   jax.ShapeDtypeStruct((nb, s, D_MODEL), F32),
                   jax.ShapeDtypeStruct((nb, s, D_MODEL), BF16),
                   jax.ShapeDtypeStruct((nb, nblk, D_MODEL, tb), BF16),
                   jax.ShapeDtypeStruct((nb, ATT_HEADS, nblk, s), F32)],
        scratch_shapes=[pltpu.VMEM((nblk, D_MODEL), F32), pltpu.VMEM((tb, D_MODEL), F32)],
        compiler_params=_params(("parallel", "arbitrary")),
        name="qk",
    )(p3, p3, p3, cos, sin, qw, kw)


def _attn_kernel(q_ref, k_ref, vt_ref, sel_ref, o_ref):
    tb = MOBA_BLOCK
    hd = ATT_HEADDIM
    i = pl.program_id(2)
    nh = LANES // hd
    q = q_ref[0]
    lane_head = _iota((tb, LANES), 1) // hd
    causal = _iota((tb, tb), 0) <= _iota((tb, tb), 1)
    qhs = [jnp.where(lane_head == hh, q, jnp.zeros_like(q)) for hh in range(nh)]

    kd = k_ref[0, pl.ds(pl.multiple_of(i * tb, tb), tb), :]
    state = []
    for hh in range(nh):
        s = jnp.where(causal, _dot_nt(kd, qhs[hh]), NEG_INF)
        m = jnp.max(s, axis=0, keepdims=True)
        p = jnp.exp(s - m)
        state += [m, jnp.sum(p, axis=0, keepdims=True),
                  _dot(vt_ref[0, i, hh * hd:(hh + 1) * hd, :], p.astype(BF16))]

    first_blk = _iota((2 * tb, tb), 0) < tb

    def body(jj, carry):
        j = 2 * jj
        kj = k_ref[0, pl.ds(pl.multiple_of(j * tb, 2 * tb), 2 * tb), :]
        ss = [_dot_nt(kj, qhs[hh]) for hh in range(nh)]
        new = []
        for hh in range(nh):
            m, l, acc = carry[3 * hh:3 * hh + 3]
            on = jnp.where(first_blk, sel_ref[0, hh, pl.ds(j, 1), :], sel_ref[0, hh, pl.ds(j + 1, 1), :]) > 0.5
            s = jnp.where(on, ss[hh], NEG_INF)
            m_new = jnp.maximum(m, jnp.max(s, axis=0, keepdims=True))
            p = jnp.exp(s - m_new)
            alpha = jnp.exp(m - m_new)
            vj = jnp.concatenate([vt_ref[0, j, hh * hd:(hh + 1) * hd, :],
                                  vt_ref[0, j + 1, hh * hd:(hh + 1) * hd, :]], axis=1)
            new += [m_new, alpha * l + jnp.sum(p, axis=0, keepdims=True), acc * alpha + _dot(vj, p.astype(BF16))]
        return tuple(new)

    state = lax.fori_loop(0, (i + 1) // 2, body, tuple(state))
    outs = [state[3 * hh + 2] / state[3 * hh + 1] for hh in range(nh)]
    o_ref[0] = jnp.concatenate(outs, axis=0).T.astype(BF16)


def _attn(qa, ka, vt, sel):
    nb, s, _ = qa.shape
    tb = MOBA_BLOCK
    nblk = s // tb
    npair = D_MODEL // LANES
    hp = LANES // ATT_HEADDIM
    return pl.pallas_call(
        _attn_kernel,
        grid=(nb, npair, nblk),
        in_specs=[pl.BlockSpec((1, tb, LANES), lambda b, p, i: (b, i, p)),
                  pl.BlockSpec((1, s, LANES), lambda b, p, i: (b, 0, p)),
                  pl.BlockSpec((1, nblk, LANES, tb), lambda b, p, i: (b, 0, p, 0)),
                  pl.BlockSpec((1, hp, nblk, tb), lambda b, p, i: (b, p, 0, i))],
        out_specs=pl.BlockSpec((1, tb, LANES), lambda b, p, i: (b, i, p)),
        out_shape=jax.ShapeDtypeStruct((nb, s, D_MODEL), BF16),
        compiler_params=_params(("parallel", "parallel", "arbitrary")),
        name="attn",
    )(qa, ka, vt, sel)


def _qk_s_kernel(q_ref, k_ref, cos_ref, sin_ref, qw_ref, kw_ref, qr_ref, kr_ref):
    rows = q_ref.shape[0]
    cos = cos_ref[...]
    sin = sin_ref[...]
    bd, first = _head_consts(rows)
    for cb in range(D_MODEL // LANES):
        sl = slice(cb * LANES, (cb + 1) * LANES)
        qr_ref[:, sl] = _head_norm_rope(q_ref[:, sl], qw_ref[...], cos, sin, bd, first)
        kr_ref[:, sl] = _head_norm_rope(k_ref[:, sl], kw_ref[...], cos, sin, bd, first)


def _qk_s(p2, cos, sin, qw, kw):
    n = p2.shape[0]
    col = lambda cidx: pl.BlockSpec((n, D_MODEL), lambda i: (0, cidx))
    full = lambda r: pl.BlockSpec((r, LANES), lambda i: (0, 0))
    return pl.pallas_call(
        _qk_s_kernel,
        grid=(1,),
        in_specs=[col(COL_Q), col(COL_K), full(n), full(n), full(1), full(1)],
        out_specs=[pl.BlockSpec((n, D_MODEL), lambda i: (0, 0))] * 2,
        out_shape=[jax.ShapeDtypeStruct((n, D_MODEL), F32)] * 2,
        compiler_params=_params(("arbitrary",)),
        name="qk_s",
    )(p2, p2, cos, sin, qw, kw)


def _dec_gate_kernel(pt_ref, q_ref, k0_ref, k1_ref, s_ref, sel_ref, qb_scr, gate_scr, *, nblk):
    del pt_ref
    j = pl.program_id(1)
    lane = _iota((ATT_HEADS, LANES), 1)

    @pl.when(j == 0)
    def _():
        qb_scr[...] = jnp.broadcast_to(q_ref[0], qb_scr.shape)
        gate_scr[...] = jnp.full_like(gate_scr, NEG_INF)

    for h in range(ATT_HEADS):
        qh = qb_scr[h]
        s_ref[0, 0, h:h + 1, 0:PAGE_SIZE] = jnp.sum(k0_ref[h] * qh, axis=0, keepdims=True)
        s_ref[0, 0, h:h + 1, PAGE_SIZE:] = jnp.sum(k1_ref[h] * qh, axis=0, keepdims=True)
    s = s_ref[0, 0]
    g = jnp.sum(s, axis=-1, keepdims=True) * (1.0 / MOBA_BLOCK)
    gate_scr[...] = jnp.where(lane == j, g, gate_scr[...])
    s_ref[0, 0] = s * (ATT_HEADDIM ** -0.5)

    @pl.when(j == nblk - 1)
    def _():
        g = gate_scr[...]
        chosen = jnp.zeros((ATT_HEADS, LANES), F32)
        for _ in range(MOBA_TOPK):
            m = jnp.max(g, axis=-1, keepdims=True)
            idx = jnp.min(jnp.where(g == m, lane, LANES), axis=-1, keepdims=True)
            pick = lane == idx
            chosen = jnp.where(pick & (lane < nblk), 1.0, chosen)
            g = jnp.where(pick, NEG_INF, g)
        sel_ref[0] = chosen


def _dec_attend_kernel(pt_ref, fetch_ref, flag_ref, s_ref, sel_ref, q_ref, kn_ref, vn_ref, v0_ref, v1_ref,
                       o_ref, m_scr, l_scr, acc_scr):
    del pt_ref, fetch_ref
    b = pl.program_id(0)
    j = pl.program_id(1)

    @pl.when(j == 0)
    def _():
        for h in range(ATT_HEADS):
            m_scr[h:h + 1, :] = jnp.sum(q_ref[0, h] * kn_ref[0, h], axis=0, keepdims=True) * (ATT_HEADDIM ** -0.5)
        l_scr[...] = jnp.ones_like(l_scr)
        acc_scr[...] = vn_ref[0]

    @pl.when(flag_ref[b, j] == 1)
    def _():
        lane = _iota((ATT_HEADS, LANES), 1)
        on = jnp.sum(jnp.where(lane == j, sel_ref[0], 0.0), axis=-1, keepdims=True) > 0.5
        s = jnp.where(on, s_ref[0, 0], NEG_INF)
        m = m_scr[...]
        m_new = jnp.maximum(m, jnp.max(s, axis=-1, keepdims=True))
        p = jnp.exp(s - m_new)
        alpha = jnp.exp(m - m_new)
        m_scr[...] = m_new
        l_scr[...] = alpha * l_scr[...] + jnp.sum(p, axis=-1, keepdims=True)
        for h in range(ATT_HEADS):
            pv = (jnp.sum(v0_ref[h] * p[h:h + 1, 0:PAGE_SIZE], axis=-1, keepdims=True)
                  + jnp.sum(v1_ref[h] * p[h:h + 1, PAGE_SIZE:], axis=-1, keepdims=True))
            acc_scr[h] = alpha[h:h + 1, :] * acc_scr[h] + pv

    @pl.when(j == pl.num_programs(1) - 1)
    def _():
        for h in range(ATT_HEADS):
            o_ref[0, h] = acc_scr[h] / l_scr[h:h + 1, :]


def _dec_attn(q_rot, k_new, v_new, pool_k, pool_v, page_table):
    nb = q_rot.shape[0]
    n_pages = page_table.shape[1]
    per = MOBA_BLOCK // PAGE_SIZE
    assert per == 2
    nblk = n_pages // per
    assert MOBA_TOPK <= nblk <= LANES
    col_shape = (1, ATT_HEADS, ATT_HEADDIM, 1)
    page_shape = (None, ATT_HEADS, ATT_HEADDIM, PAGE_SIZE)
    s_shape = (1, 1, ATT_HEADS, MOBA_BLOCK)

    gate_spec = pltpu.PrefetchScalarGridSpec(
        num_scalar_prefetch=1,
        grid=(nb, nblk),
        in_specs=[pl.BlockSpec(col_shape, lambda b, j, pt: (b, 0, 0, 0)),
                  pl.BlockSpec(page_shape, lambda b, j, pt: (pt[b, per * j], 0, 0, 0)),
                  pl.BlockSpec(page_shape, lambda b, j, pt: (pt[b, per * j + 1], 0, 0, 0))],
        out_specs=[pl.BlockSpec(s_shape, lambda b, j, pt: (b, j, 0, 0)),
                   pl.BlockSpec((1, ATT_HEADS, LANES), lambda b, j, pt: (b, 0, 0))],
        scratch_shapes=[pltpu.VMEM((ATT_HEADS, ATT_HEADDIM, PAGE_SIZE), F32), pltpu.VMEM((ATT_HEADS, LANES), F32)])
    scores, sel = pl.pallas_call(
        functools.partial(_dec_gate_kernel, nblk=nblk),
        grid_spec=gate_spec,
        out_shape=[jax.ShapeDtypeStruct((nb, nblk, ATT_HEADS, MOBA_BLOCK), F32),
                   jax.ShapeDtypeStruct((nb, ATT_HEADS, LANES), F32)],
        compiler_params=_params(("parallel", "arbitrary")),
        name="dec_gate",
    )(page_table, q_rot, pool_k, pool_k)

    blk = jnp.arange(nblk, dtype=I32)
    flag = (jnp.max(sel[:, :, :nblk], axis=1) > 0.5).astype(I32)
    fetch = lax.cummax(jnp.where(flag == 1, blk[None, :], 0), axis=1)

    col = pl.BlockSpec(col_shape, lambda b, j, pt, fe, fl: (b, 0, 0, 0))

    def page(half):
        return pl.BlockSpec(page_shape, lambda b, j, pt, fe, fl: (pt[b, per * fe[b, j] + half], 0, 0, 0))

    attend_spec = pltpu.PrefetchScalarGridSpec(
        num_scalar_prefetch=3,
        grid=(nb, nblk),
        in_specs=[pl.BlockSpec(s_shape, lambda b, j, pt, fe, fl: (b, fe[b, j], 0, 0)),
                  pl.BlockSpec((1, ATT_HEADS, LANES), lambda b, j, pt, fe, fl: (b, 0, 0)),
                  col, col, col, page(0), page(1)],
        out_specs=col,
        scratch_shapes=[pltpu.VMEM((ATT_HEADS, 1), F32), pltpu.VMEM((ATT_HEADS, 1), F32),
                        pltpu.VMEM((ATT_HEADS, ATT_HEADDIM, 1), F32)])
    return pl.pallas_call(
        _dec_attend_kernel,
        grid_spec=attend_spec,
        out_shape=jax.ShapeDtypeStruct((nb, ATT_HEADS, ATT_HEADDIM, 1), F32),
        compiler_params=_params(("parallel", "arbitrary")),
        name="dec_attend",
    )(page_table, fetch, flag, scores, sel, q_rot, k_new, v_new, pool_v, pool_v)


def _headnorm_kernel(x_ref, w_ref, o_ref):
    for h in range(MEM_HEADS):
        sl = slice(h * MEM_HEADDIM, (h + 1) * MEM_HEADDIM)
        x = x_ref[:, sl]
        ms = jnp.mean(x * x, axis=-1, keepdims=True)
        o_ref[:, sl] = x * lax.rsqrt(ms + EPS) * w_ref[...]


def _headnorm(x2, w, tm):
    t = x2.shape[0]
    return pl.pallas_call(
        _headnorm_kernel,
        grid=(t // tm,),
        in_specs=[pl.BlockSpec((tm, D_MODEL), lambda i: (i, 0)), pl.BlockSpec((1, MEM_HEADDIM), lambda i: (0, 0))],
        out_specs=pl.BlockSpec((tm, D_MODEL), lambda i: (i, 0)),
        out_shape=jax.ShapeDtypeStruct((t, D_MODEL), F32),
        compiler_params=_params(("parallel",)),
        name="headnorm",
    )(x2, w)


def _mem_kernel(q_ref, k_ref, v_ref, qw_ref, o_ref):
    kb = k_ref[0].astype(BF16)
    vb = v_ref[0].astype(BF16)
    for h in range(MEM_HEADS):
        sl = slice(h * MEM_HEADDIM, (h + 1) * MEM_HEADDIM)
        q = q_ref[0, :, sl]
        ms = jnp.mean(q * q, axis=-1, keepdims=True)
        qn = (q * lax.rsqrt(ms + EPS) * qw_ref[...]).astype(BF16)
        s = _dot_nt(qn, kb[:, sl]) * (MEM_HEADDIM ** -0.5)
        e = jnp.exp(s - jnp.max(s, axis=-1, keepdims=True))
        p = e / jnp.sum(e, axis=-1, keepdims=True)
        o_ref[0, :, sl] = _dot(p.astype(BF16), vb[:, sl]).astype(BF16)


def _mem_attend(q3, qcol, mk, mv, qw, tq):
    nb, s, _ = q3.shape
    m = mk.shape[1]
    return pl.pallas_call(
        _mem_kernel,
        grid=(nb, s // tq),
        in_specs=[pl.BlockSpec((1, tq, D_MODEL), lambda b, i: (b, i, qcol)),
                  pl.BlockSpec((1, m, D_MODEL), lambda b, i: (b, 0, 0)),
                  pl.BlockSpec((1, m, D_MODEL), lambda b, i: (b, 0, 0)),
                  pl.BlockSpec((1, MEM_HEADDIM), lambda b, i: (0, 0))],
        out_specs=pl.BlockSpec((1, tq, D_MODEL), lambda b, i: (b, i, 0)),
        out_shape=jax.ShapeDtypeStruct((nb, s, D_MODEL), BF16),
        compiler_params=_params(("parallel", "parallel")),
        name="mem_attend",
    )(q3, mk, mv, qw)


def _merge_kernel(ys_ref, oa_ref, om_ref, g1_ref, g2_ref, g3_ref, x_ref, ws_ref, wa_ref, wm_ref, wo_ref, nw_ref,
                  x1_ref, hf_ref):
    merged = (_sigmoid(g1_ref[...]) * _dot(ys_ref[...], ws_ref[...])
              + _sigmoid(g2_ref[...]) * _dot(oa_ref[...], wa_ref[...])
              + _sigmoid(g3_ref[...]) * _dot(om_ref[...], wm_ref[...]))
    x1 = x_ref[...] + _dot(merged.astype(BF16), wo_ref[...])
    x1_ref[...] = x1
    ms = jnp.mean(x1 * x1, axis=-1, keepdims=True)
    hf_ref[...] = (x1 * lax.rsqrt(ms + EPS) * nw_ref[...]).astype(BF16)


def _merge(y_ssd, o_att, o_mem, p2, x2, ws, wa, wm, wo, nw, tm):
    t = x2.shape[0]
    tok = lambda w, c=0: pl.BlockSpec((tm, w), lambda i: (i, c))
    res = lambda a: pl.BlockSpec(a.shape, lambda i: (0, 0))
    return pl.pallas_call(
        _merge_kernel,
        grid=(t // tm,),
        in_specs=[tok(SSD_INNER), tok(D_MODEL), tok(D_MODEL),
                  tok(D_MODEL, COL_G), tok(D_MODEL, COL_G + 1), tok(D_MODEL, COL_G + 2), tok(D_MODEL),
                  res(ws), res(wa), res(wm), res(wo), res(nw)],
        out_specs=[tok(D_MODEL), tok(D_MODEL)],
        out_shape=[jax.ShapeDtypeStruct((t, D_MODEL), F32), jax.ShapeDtypeStruct((t, D_MODEL), BF16)],
        compiler_params=_params(("parallel",)),
        name="merge",
    )(y_ssd, o_att, o_mem, p2, p2, p2, x2, ws, wa, wm, wo, nw)


def _top16(src_scr, n_rows, tp):
    r_id = _iota((n_rows, tp), 0)
    slot = _iota((PEER_TOPK, tp), 0)

    def body(r, carry):
        vals, idxs = carry
        sc = src_scr[...]
        m = jnp.max(sc, axis=0, keepdims=True)
        idx = jnp.min(jnp.where(sc == m, r_id, n_rows), axis=0, keepdims=True)
        src_scr[...] = jnp.where(r_id == idx, NEG_INF, sc)
        return jnp.where(slot == r, m, vals), jnp.where(slot == r, idx, idxs)

    return lax.fori_loop(0, PEER_TOPK, body, (jnp.zeros((PEER_TOPK, tp), F32), jnp.zeros((PEER_TOPK, tp), I32)))


def _peer_sel_kernel(hf_ref, wqt_ref, keys_ref, a_ref, b_ref, g_ref,
                     qt_scr, sc_scr, sv_scr, si_scr, cand_scr, oa_scr, ob_scr, og_scr):
    tp = hf_ref.shape[0]
    nk = PEER_NKEYS
    kk = PEER_TOPK
    qt_scr[...] = _dot_nt(wqt_ref[...], hf_ref[...])

    def half(hc, carry):
        q_hc = qt_scr[pl.ds(pl.multiple_of(hc * nk, nk), nk), :]
        sc_scr[...] = _dot3(keys_ref[hc], q_hc, _dot)
        vals, idxs = _top16(sc_scr, nk, tp)
        sv_scr[hc] = vals
        si_scr[hc] = idxs
        return carry

    lax.fori_loop(0, 2 * PEER_HEADS, half, 0)

    def head(h, carry):
        sv1 = sv_scr[2 * h]
        sv2 = sv_scr[2 * h + 1]
        si1 = si_scr[2 * h]
        si2 = si_scr[2 * h + 1]
        cand_scr[0:kk, :] = sv1[0:1, :] + sv2
        b8 = _iota((SUBLANES, tp), 0)
        for a in range(1, SUBLANES):
            c = sv1[a:a + 1, :] + sv2[0:SUBLANES, :]
            n_b = kk // (a + 1)
            cand_scr[SUBLANES * (a + 1):SUBLANES * (a + 2), :] = c if n_b >= SUBLANES else jnp.where(b8 < n_b, c, NEG_INF)
        cand_scr[PEER_NCAND - SUBLANES:, :] = sv1[SUBLANES:, :] + sv2[0:1, :]
        fv, fr = _top16(cand_scr, PEER_NCAND, tp)
        mid = fr - kk
        fa = jnp.where(fr < kk, 0, jnp.where(fr < PEER_NCAND - SUBLANES, 1 + (mid >> 3), fr - (PEER_NCAND - 2 * SUBLANES)))
        fb = jnp.where(fr < kk, fr, jnp.where(fr < PEER_NCAND - SUBLANES, mid & (SUBLANES - 1), 0))
        e1 = jnp.zeros((kk, tp), I32)
        e2 = jnp.zeros((kk, tp), I32)
        for a in range(kk):
            e1 = e1 + jnp.where(fa == a, si1[a:a + 1, :], 0)
            e2 = e2 + jnp.where(fb == a, si2[a:a + 1, :], 0)
        e = jnp.exp(fv - fv[0:1, :])
        rows = pl.ds(pl.multiple_of(h * kk, kk), kk)
        oa_scr[rows, :] = e1.astype(F32)
        ob_scr[rows, :] = e2.astype(F32)
        og_scr[rows, :] = e / jnp.sum(e, axis=0, keepdims=True)
        return carry

    lax.fori_loop(0, PEER_HEADS, head, 0)
    a_ref[...] = oa_scr[...].T
    b_ref[...] = ob_scr[...].T
    g_ref[...] = og_scr[...].T


def _peer_sel(hf, wqt, keys, tp):
    t = hf.shape[0]
    kk = PEER_TOPK
    npick = PEER_HEADS * kk
    out = pl.BlockSpec((tp, npick), lambda i: (i, 0))
    return pl.pallas_call(
        _peer_sel_kernel,
        grid=(t // tp,),
        in_specs=[pl.BlockSpec((tp, D_MODEL), lambda i: (i, 0)),
                  pl.BlockSpec(wqt.shape, lambda i: (0, 0)),
                  pl.BlockSpec(keys.shape, lambda i: (0, 0, 0))],
        out_specs=[out, out, out],
        out_shape=[jax.ShapeDtypeStruct((t, npick), F32)] * 3,
        scratch_shapes=[pltpu.VMEM((wqt.shape[0], tp), F32),
                        pltpu.VMEM((PEER_NKEYS, tp), F32),
                        pltpu.VMEM((2 * PEER_HEADS, kk, tp), F32),
                        pltpu.VMEM((2 * PEER_HEADS, kk, tp), I32),
                        pltpu.VMEM((PEER_NCAND, tp), F32),
                        pltpu.VMEM((npick, tp), F32),
                        pltpu.VMEM((npick, tp), F32),
                        pltpu.VMEM((npick, tp), F32)],
        compiler_params=_params(("parallel",)),
        name="peer_sel",
    )(hf, wqt, keys)


def _peer_kernel(hf_ref, a_ref, b_ref, g_ref, x1_ref, u_ref, v_ref, o_ref, g_scr, *, tm, te):
    nk = PEER_NKEYS
    e = pl.program_id(1)
    n1 = te // nk

    @pl.when(e == 0)
    def _():
        o_ref[...] = x1_ref[...]
        sub = _iota((nk, LANES), 0).astype(F32)

        def toks(t8, carry):
            for u in range(SUBLANES):
                t = t8 * SUBLANES + u
                row = pl.ds(t, 1)
                y = jnp.where(sub == a_ref[row, :], g_ref[row, :], 0.0).astype(BF16)
                z = jnp.where(sub == b_ref[row, :], 1.0, 0.0).astype(BF16)
                g_scr[pl.ds(pl.multiple_of(t * nk, nk), nk), :] = _dot_nt(y, z)
            return carry

        lax.fori_loop(0, tm // SUBLANES, toks, 0)

    hf = hf_ref[...]
    half = te // 2
    for c in range(2):
        act = _dot_nt(hf, u_ref[c * half:(c + 1) * half, :])
        act = 0.5 * act * (1.0 + lax.erf(act * (2.0 ** -0.5)))
        ws = []
        for k in range(half // nk):
            gk = g_scr[pl.ds(e * n1 + c * (half // nk) + k, tm, stride=nk), :]
            ws.append((act[:, k * nk:(k + 1) * nk] * gk).astype(BF16))
        o_ref[...] += _dot(jnp.concatenate(ws, axis=1), v_ref[c * half:(c + 1) * half, :])


def _peer(hf, a, b, g, x1, u, v, tm, te):
    t = hf.shape[0]
    n_exp = u.shape[0]
    npick = a.shape[1]
    tok = lambda w: pl.BlockSpec((tm, w), lambda i, e: (i, 0))
    exp = pl.BlockSpec((te, D_MODEL), lambda i, e: (e, 0))
    return pl.pallas_call(
        functools.partial(_peer_kernel, tm=tm, te=te),
        grid=(t // tm, n_exp // te),
        in_specs=[tok(D_MODEL), tok(npick), tok(npick), tok(npick), tok(D_MODEL), exp, exp],
        out_specs=tok(D_MODEL),
        out_shape=jax.ShapeDtypeStruct((t, D_MODEL), F32),
        scratch_shapes=[pltpu.VMEM((tm * PEER_NKEYS, PEER_NKEYS), F32)],
        compiler_params=_params(("parallel", "arbitrary")),
        name="peer",
    )(hf, a, b, g, x1, u, v)


def _pack_w_in(w_in):
    n0 = SSD_INNER + SSD_CONV_DIM
    w_main = jnp.concatenate([w_in[:, :n0], w_in[:, n0 + SSD_HEADS:]], axis=1).astype(BF16)
    w_dt = jnp.pad(w_in[:, n0:n0 + SSD_HEADS], ((0, 0), (0, LANES - SSD_HEADS))).astype(BF16)
    return w_main, w_dt


def _pad_lanes(v):
    return jnp.pad(v, (0, LANES - v.shape[0]))[None]


def kernel(x_prompt, x_sample, mem_prompt, cache_k, cache_v, page_table, cache_mem_k, cache_mem_v, state_ssm, state_conv, norm1_w, w_in, conv_w, conv_b, dt_bias, a_log, d_skip, ssd_norm_w, q_norm_w, k_norm_w, mem_norm_w, w_mem_k, w_mem_v, mem_qn_w, mem_kn_w, w_br_ssd, w_br_attn, w_br_mem, w_out, norm2_w, peer_wq, peer_keys, peer_u, peer_v):
    bp, sp, _ = x_prompt.shape
    bs, ss, _ = x_sample.shape
    assert ss == 1 and cache_k.shape[0] == 1
    tp, ts = bp * sp, bs * ss

    w_main, w_dt = _pack_w_in(w_in[0])
    nw1 = norm1_w[0][None]
    ssd_w = (conv_w[0], conv_b[0][None], _pad_lanes(dt_bias[0]), _pad_lanes(a_log[0]),
             jnp.repeat(d_skip[0], SSD_HEADDIM)[None], ssd_norm_w[0][None])
    qw2 = jnp.tile(q_norm_w[0], LANES // ATT_HEADDIM)[None]
    kw2 = jnp.tile(k_norm_w[0], LANES // ATT_HEADDIM)[None]
    w_mem_kv = jnp.concatenate([w_mem_k[0], w_mem_v[0]], axis=1).astype(BF16)
    merge_w = (w_br_ssd[0].astype(BF16), w_br_attn[0].astype(BF16), w_br_mem[0].astype(BF16),
               w_out[0].astype(BF16), norm2_w[0][None])
    wqt = peer_wq[0].T.astype(BF16)
    keys = peer_keys[0].reshape(2 * PEER_HEADS, PEER_NKEYS, -1)
    u_bf = peer_u[0].astype(BF16)
    v_bf = peer_v[0].astype(BF16)
    mem_qw = mem_qn_w[0][None]

    xp = x_prompt.reshape(tp, D_MODEL)
    p_main = _proj(xp, nw1, w_main, 1024, 1024)
    p_dt = _proj(xp, nw1, w_dt, 1024, LANES)
    p3 = p_main.reshape(bp, sp, N_MAIN)
    y_ssd, ssm_p = _ssd(p3, p_dt.reshape(bp, sp, LANES), jnp.zeros((bp, SUBLANES, SSD_CONV_DIM), F32),
                        jnp.zeros((bp, SSD_HEADS, SSD_HEADDIM, SSD_STATE), F32), *ssd_w, SSD_CHUNK)
    cos, sin = _rope_tables(jnp.arange(sp, dtype=I32))
    qa, k_rot, ka, vt, sel = _qk(p3, cos, sin, qw2, kw2)
    o_att = _attn(qa, ka, vt, sel)
    n_mem = mem_prompt.shape[1]
    kv = _proj(mem_prompt.reshape(bp * n_mem, D_MODEL), mem_norm_w[0][None], w_mem_kv, 256, D_MODEL)
    mem_k = _headnorm(kv, mem_kn_w[0][None], 256)
    mem_v = kv[:, D_MODEL:]
    o_mem = _mem_attend(p3, COL_MQ, mem_k.reshape(bp, n_mem, D_MODEL), mem_v.reshape(bp, n_mem, D_MODEL), mem_qw, 512)
    x1, hf = _merge(y_ssd.reshape(tp, SSD_INNER), o_att.reshape(tp, D_MODEL), o_mem.reshape(tp, D_MODEL),
                    p_main, xp, *merge_w, 512)
    pa, pb, pg = _peer_sel(hf, wqt, keys, 256)
    y_p = _peer(hf, pa, pb, pg, x1, u_bf, v_bf, 256, 1024)

    xs = x_sample.reshape(ts, D_MODEL)
    ps_main = _proj(xs, nw1, w_main, ts, 1024)
    ps_dt = _proj(xs, nw1, w_dt, ts, LANES)
    ps3 = ps_main.reshape(bs, 1, N_MAIN)
    conv_prev8 = jnp.pad(state_conv[0], ((0, 0), (SUBLANES - (SSD_CONV - 1), 0), (0, 0)))
    ys_ssd, ssm_s = _ssd(ps3, ps_dt.reshape(bs, 1, LANES), conv_prev8, state_ssm[0], *ssd_w, 1)
    cos_s, sin_s = _rope_tables(jnp.full((ts,), PAST_LEN, I32))
    q_rot_s, k_rot_s = _qk_s(ps_main, cos_s, sin_s, qw2, kw2)
    v_s = ps_main[:, COL_V * D_MODEL:(COL_V + 1) * D_MODEL]
    cols = lambda v: v.reshape(bs, ATT_HEADS, ATT_HEADDIM, 1)
    pool = lambda c: jnp.transpose(c[0], (0, 2, 3, 1))
    o_att_s = _dec_attn(cols(q_rot_s), cols(k_rot_s), cols(v_s), pool(cache_k), pool(cache_v), page_table)
    mq8 = jnp.broadcast_to(ps3[:, :, COL_MQ * D_MODEL:(COL_MQ + 1) * D_MODEL], (bs, SUBLANES, D_MODEL))
    o_mem_s = _mem_attend(mq8, 0, cache_mem_k[0].reshape(bs, n_mem, D_MODEL),
                          cache_mem_v[0].reshape(bs, n_mem, D_MODEL), mem_qw, SUBLANES)[:, 0]
    x1_s, hf_s = _merge(ys_ssd.reshape(ts, SSD_INNER), o_att_s.reshape(ts, D_MODEL).astype(BF16), o_mem_s,
                        ps_main, xs, *merge_w, ts)
    pad_rows = lambda v: jnp.pad(v, ((0, LANES - ts), (0, 0)))
    hf_sp = pad_rows(hf_s)
    sa, sb, sg = _peer_sel(hf_sp, wqt, keys, LANES)
    y_s = _peer(hf_sp, sa, sb, sg, pad_rows(x1_s), u_bf, v_bf, LANES, 1024)[:ts]

    kv_shape = (1, bp, sp, ATT_HEADS, ATT_HEADDIM)
    mem_shape = (1, bp, n_mem, MEM_HEADS, MEM_HEADDIM)
    xbc_cols = slice(COL_X * 1024, COL_X * 1024 + SSD_CONV_DIM)
    conv_p = p3[:, sp - (SSD_CONV - 1):, xbc_cols]
    conv_s = jnp.concatenate([state_conv[0][:, 1:], ps3[:, :, xbc_cols]], axis=1)
    return (y_p.reshape(bp, sp, D_MODEL), y_s.reshape(bs, ss, D_MODEL),
            k_rot.reshape(kv_shape), p3[:, :, COL_V * 1024:(COL_V + 1) * 1024].reshape(kv_shape),
            mem_k.reshape(mem_shape), mem_v.reshape(mem_shape),
            ssm_p[None], conv_p[None],
            k_rot_s.reshape(1, bs, ss, ATT_HEADS, ATT_HEADDIM), v_s.reshape(1, bs, ss, ATT_HEADS, ATT_HEADDIM),
            ssm_s[None], conv_s[None])
```

```python
import functools
import math

import jax
import jax.numpy as jnp
from jax import lax
from jax.experimental import pallas as pl
from jax.experimental.pallas import tpu as pltpu

F32 = jnp.float32
BF16 = jnp.bfloat16
I32 = jnp.int32
EPS = 1e-6
NEG_INF = float("-inf")

D_MODEL = 1024
SSD_INNER = 2048
SSD_HEADS = 32
SSD_HEADDIM = 64
SSD_GROUPS = 8
SSD_STATE = 128
SSD_CONV = 4
SSD_CHUNK = 128
SSD_CONV_DIM = 4096
ATT_HEADS = 16
ATT_HEADDIM = 64
MOBA_BLOCK = 256
MOBA_TOPK = 3
ROPE_THETA = 10000.0
MEM_HEADS = 4
MEM_HEADDIM = 256
PEER_HEADS = 8
PEER_NKEYS = 128
PEER_TOPK = 16
PEER_BUILD_TOKENS = 16
PEER_NCAND = 80
PAST_LEN = 16384
PAGE_SIZE = 128
DEC_BLOCKS_PER_STEP = 4
DEC_ATTEND_BLOCKS_PER_STEP = 2

LANES = 128
SUBLANES = 8
VMEM_LIMIT = 56 * 1024 * 1024

COL_Z, COL_X, COL_BC, COL_Q, COL_K, COL_V, COL_MQ, COL_G = 0, 2, 4, 6, 7, 8, 9, 10
N_MAIN = 13 * 1024


def _params(sem):
    return pltpu.CompilerParams(dimension_semantics=sem, vmem_limit_bytes=VMEM_LIMIT)


def _dot(a, b):
    return jnp.dot(a, b, preferred_element_type=F32)


def _dot_nt(a, b):
    return lax.dot_general(a, b, (((1,), (1,)), ((), ())), preferred_element_type=F32)


def _dot_tn(a, b):
    return lax.dot_general(a, b, (((0,), (0,)), ((), ())), preferred_element_type=F32)


def _split(a):
    hi = a.astype(BF16)
    lo = (a - hi.astype(F32)).astype(BF16)
    return hi, lo


def _dot3(a, b, dot):
    ah, al = _split(a)
    bh, bl = _split(b)
    return dot(ah, bh) + (dot(ah, bl) + dot(al, bh))


def _sigmoid(x):
    return 1.0 / (1.0 + jnp.exp(-x))


def _iota(shape, axis):
    return lax.broadcasted_iota(I32, shape, axis)


def _tree(x, op):
    while x.shape[0] > 1:
        h = x.shape[0] // 2
        x = op(x[:h], x[h:])
    return x[0]


def _proj_kernel(x_ref, nw_ref, w_ref, o_ref, h_scr):
    @pl.when(pl.program_id(1) == 0)
    def _():
        x = x_ref[...]
        ms = jnp.mean(x * x, axis=-1, keepdims=True)
        h_scr[...] = (x * lax.rsqrt(ms + EPS) * nw_ref[...]).astype(BF16)

    o_ref[...] = _dot(h_scr[...], w_ref[...])


def _proj(x, nw, w, tm, tn):
    t, d = x.shape
    n = w.shape[1]
    return pl.pallas_call(
        _proj_kernel,
        grid=(t // tm, n // tn),
        in_specs=[pl.BlockSpec((tm, d), lambda i, j: (i, 0)),
                  pl.BlockSpec((1, d), lambda i, j: (0, 0)),
                  pl.BlockSpec((d, tn), lambda i, j: (0, j))],
        out_specs=pl.BlockSpec((tm, tn), lambda i, j: (i, j)),
        out_shape=jax.ShapeDtypeStruct((t, n), F32),
        scratch_shapes=[pltpu.VMEM((tm, d), BF16)],
        compiler_params=_params(("parallel", "arbitrary")),
        name="proj",
    )(x, nw, w)


def _ssd_kernel(z_ref, x_ref, bc_ref, dt_ref, cprev_ref, h0_ref, cw_ref, cb_ref, dtb_ref, alog_ref,
                dsk_ref, nw_ref, y_ref, hl_ref, buf, st, ycat, *, rows):
    L = SSD_CHUNK
    c = pl.program_id(1)

    @pl.when(c == 0)
    def _():
        buf[0:SUBLANES, :] = cprev_ref[0]
        st[...] = h0_ref[0].reshape(st.shape)

    if rows < L:
        buf[SUBLANES:, :] = jnp.zeros((L, SSD_CONV_DIM), F32)
    buf[SUBLANES:SUBLANES + rows, 0:SSD_INNER] = x_ref[0]
    buf[SUBLANES:SUBLANES + rows, SSD_INNER:] = bc_ref[0]

    acc = buf[SUBLANES:SUBLANES + L, :] * cw_ref[SSD_CONV - 1:SSD_CONV, :]
    for j in range(SSD_CONV - 1):
        off = SUBLANES - (SSD_CONV - 1) + j
        acc = acc + buf[off:off + L, :] * cw_ref[j:j + 1, :]
    acc = acc + cb_ref[...]
    xbc = acc * _sigmoid(acc)
    buf[0:SUBLANES, :] = buf[L:L + SUBLANES, :]

    xs = xbc[:, :SSD_INNER]
    n_gn = SSD_GROUPS * SSD_STATE
    bm = xbc[:, SSD_INNER:SSD_INNER + n_gn].astype(BF16)
    cm = xbc[:, SSD_INNER + n_gn:].astype(BF16)

    dt_raw = dt_ref[0] if rows == L else jnp.broadcast_to(dt_ref[0], (L, LANES))
    pre = dt_raw + dtb_ref[...]
    dt = jnp.maximum(pre, 0.0) + jnp.log1p(jnp.exp(-jnp.abs(pre)))
    row_id = _iota((L, LANES), 0)
    dt = jnp.where(row_id < rows, dt, 0.0)
    a = -jnp.exp(alog_ref[...])
    da = dt * a
    tri = _iota((L, L), 0) >= _iota((L, L), 1)
    a_cs = jnp.dot(tri.astype(F32), da, preferred_element_type=F32, precision=lax.Precision.HIGHEST)
    a_cs_t = a_cs.T
    a_last = a_cs[L - 1:L, :]
    e_cs = jnp.exp(a_cs)
    d_end = jnp.exp(a_last - a_cs)
    e_last = jnp.exp(a_last)

    head_lanes = (_iota((LANES, SSD_INNER), 1) // SSD_HEADDIM == _iota((LANES, SSD_INNER), 0))
    head_lanes = jnp.where(head_lanes, 1.0, 0.0).astype(BF16)

    def spread(v):
        p1 = v.astype(BF16)
        r1 = v - p1.astype(F32)
        p2 = r1.astype(BF16)
        p3 = (r1 - p2.astype(F32)).astype(BF16)
        return _dot(p1, head_lanes) + (_dot(p2, head_lanes) + _dot(p3, head_lanes))

    xdt_all = xs * spread(dt)
    xdt_bf = xdt_all.astype(BF16)
    xw_bf = (xdt_all * spread(d_end)).astype(BF16)
    e_cs_all = spread(e_cs)
    left = _iota((L, LANES), 1) < SSD_HEADDIM
    top = _iota((LANES, SSD_STATE), 0) < SSD_HEADDIM

    for g in range(SSD_GROUPS):
        bg = bm[:, g * SSD_STATE:(g + 1) * SSD_STATE]
        cg = cm[:, g * SSD_STATE:(g + 1) * SSD_STATE]
        cb = _dot_nt(cg, bg)
        for r in range(SSD_HEADS // SSD_GROUPS // 2):
            pair = g * (SSD_HEADS // SSD_GROUPS // 2) + r
            ps = slice(pair * LANES, (pair + 1) * LANES)
            ms_ = []
            for h in (2 * pair, 2 * pair + 1):
                decay = jnp.where(tri, jnp.exp(a_cs[:, h:h + 1] - a_cs_t[h:h + 1, :]), 0.0)
                ms_.append((cb * decay).astype(BF16))
            xp = xdt_bf[:, ps]
            yd = jnp.where(left, _dot(ms_[0], xp), _dot(ms_[1], xp))
            hp = st[pair]
            yo = _dot_nt(cg, hp.astype(BF16)) * e_cs_all[:, ps]
            keep = jnp.where(top, e_last[:, 2 * pair:2 * pair + 1], e_last[:, 2 * pair + 1:2 * pair + 2])
            st[pair] = hp * keep + _dot_tn(xw_bf[:, ps], bg)
            ycat[:, ps] = yd + yo + xs[:, ps] * dsk_ref[:, ps]

    z = z_ref[0]
    if rows < L:
        y = ycat[0:rows, :] * (z * _sigmoid(z))
    else:
        y = ycat[...] * (z * _sigmoid(z))
    ms = jnp.mean(y * y, axis=-1, keepdims=True)
    y_ref[0] = (y * lax.rsqrt(ms + EPS) * nw_ref[...]).astype(BF16)

    @pl.when(c == pl.num_programs(1) - 1)
    def _():
        hl_ref[0] = st[...].reshape(SSD_HEADS, SSD_HEADDIM, SSD_STATE)


def _ssd(p3, dt3, conv_prev8, h0, cw, cb, dtb, alog, dsk, nw, rows):
    nb, s, _ = p3.shape
    nc = s // rows
    w2 = SSD_INNER
    const = lambda shape: pl.BlockSpec(shape, lambda b, c: tuple(0 for _ in shape))
    return pl.pallas_call(
        functools.partial(_ssd_kernel, rows=rows),
        grid=(nb, nc),
        in_specs=[pl.BlockSpec((1, rows, w2), lambda b, c: (b, c, COL_Z // 2)),
                  pl.BlockSpec((1, rows, w2), lambda b, c: (b, c, COL_X // 2)),
                  pl.BlockSpec((1, rows, w2), lambda b, c: (b, c, COL_BC // 2)),
                  pl.BlockSpec((1, rows, LANES), lambda b, c: (b, c, 0)),
                  pl.BlockSpec((1, SUBLANES, SSD_CONV_DIM), lambda b, c: (b, 0, 0)),
                  pl.BlockSpec((1, SSD_HEADS, SSD_HEADDIM, SSD_STATE), lambda b, c: (b, 0, 0, 0)),
                  const((SSD_CONV, SSD_CONV_DIM)), const((1, SSD_CONV_DIM)), const((1, LANES)),
                  const((1, LANES)), const((1, SSD_INNER)), const((1, SSD_INNER))],
        out_specs=[pl.BlockSpec((1, rows, w2), lambda b, c: (b, c, 0)),
                   pl.BlockSpec((1, SSD_HEADS, SSD_HEADDIM, SSD_STATE), lambda b, c: (b, 0, 0, 0))],
        out_shape=[jax.ShapeDtypeStruct((nb, s, w2), BF16),
                   jax.ShapeDtypeStruct((nb, SSD_HEADS, SSD_HEADDIM, SSD_STATE), F32)],
        scratch_shapes=[pltpu.VMEM((SUBLANES + SSD_CHUNK, SSD_CONV_DIM), F32),
                        pltpu.VMEM((SSD_HEADS // 2, 2 * SSD_HEADDIM, SSD_STATE), F32),
                        pltpu.VMEM((SSD_CHUNK, SSD_INNER), F32)],
        compiler_params=_params(("parallel", "arbitrary")),
        name="ssd",
    )(p3, p3, p3, dt3, conv_prev8, h0, cw, cb, dtb, alog, dsk, nw)


def _rope_tables(pos):
    d = ATT_HEADDIM
    inv = 1.0 / (ROPE_THETA ** (jnp.arange(0, d, 2, dtype=F32) / d))
    ang = pos.astype(F32)[:, None] * inv[None, :]
    cos = jnp.cos(ang)
    sin = jnp.sin(ang)
    cos = jnp.concatenate([cos, cos, cos, cos], axis=-1)
    sin = jnp.concatenate([-sin, sin, -sin, sin], axis=-1)
    return cos, sin


def _head_norm_rope(x, w, cos, sin, bd, first):
    hi, lo = _split(x * x)
    ms = _dot(hi, bd) + _dot(lo, bd)
    xn = x * lax.rsqrt(ms + EPS) * w
    partner = jnp.where(first, pltpu.roll(xn, LANES - ATT_HEADDIM // 2, 1), pltpu.roll(xn, ATT_HEADDIM // 2, 1))
    return xn * cos + partner * sin


def _head_consts(rows):
    r = _iota((LANES, LANES), 0) // ATT_HEADDIM
    c = _iota((LANES, LANES), 1) // ATT_HEADDIM
    bd = jnp.where(r == c, 1.0 / ATT_HEADDIM, 0.0).astype(BF16)
    first = (_iota((rows, LANES), 1) % ATT_HEADDIM) < (ATT_HEADDIM // 2)
    return bd, first


def _qk_kernel(q_ref, k_ref, v_ref, cos_ref, sin_ref, qw_ref, kw_ref,
               qa_ref, kr_ref, ka_ref, vt_ref, sel_ref, km_scr, qr_scr, *, nblk):
    tb = MOBA_BLOCK
    i = pl.program_id(1)

    @pl.when(i == 0)
    def _():
        km_scr[...] = jnp.zeros_like(km_scr)

    cos = cos_ref[...]
    sin = sin_ref[...]
    bd, first = _head_consts(tb)
    for cb in range(D_MODEL // LANES):
        sl = slice(cb * LANES, (cb + 1) * LANES)
        qr = _head_norm_rope(q_ref[0, :, sl], qw_ref[...], cos, sin, bd, first)
        kr = _head_norm_rope(k_ref[0, :, sl], kw_ref[...], cos, sin, bd, first)
        qr_scr[:, sl] = qr
        qa_ref[0, :, sl] = (qr * (ATT_HEADDIM ** -0.5)).astype(BF16)
        kr_ref[0, :, sl] = kr
        ka_ref[0, :, sl] = kr.astype(BF16)
        k_mean = jnp.sum(kr, axis=0, keepdims=True) * (1.0 / tb)
        km_scr[:, sl] = jnp.where(_iota((nblk, LANES), 0) == i, k_mean, km_scr[:, sl])
    vt_ref[0, 0] = v_ref[0].T.astype(BF16)

    km = km_scr[...]
    lane_head = _iota((nblk, D_MODEL), 1) // ATT_HEADDIM
    kmt = jnp.concatenate([jnp.where(lane_head == h, km, 0.0) for h in range(ATT_HEADS)], axis=0)
    gate = _dot3(kmt, qr_scr[...], _dot_nt)
    n_id = _iota((nblk, tb), 0)
    valid = n_id < i
    for h in range(ATT_HEADS):
        g = jnp.where(valid, gate[h * nblk:(h + 1) * nblk, :], NEG_INF)
        chosen = jnp.zeros((nblk, tb), F32)
        for _ in range(MOBA_TOPK):
            m = jnp.max(g, axis=0, keepdims=True)
            idx = jnp.min(jnp.where(g == m, n_id, nblk), axis=0, keepdims=True)
            pick = n_id == idx
            chosen = jnp.where(pick & valid, 1.0, chosen)
            g = jnp.where(pick, NEG_INF, g)
        sel_ref[0, h] = chosen


def _qk(p3, cos, sin, qw, kw):
    nb, s, _ = p3.shape
    tb = MOBA_BLOCK
    nblk = s // tb
    col = lambda cidx: pl.BlockSpec((1, tb, D_MODEL), lambda b, i: (b, i, cidx))
    out_tok = pl.BlockSpec((1, tb, D_MODEL), lambda b, i: (b, i, 0))
    return pl.pallas_call(
        functools.partial(_qk_kernel, nblk=nblk),
        grid=(nb, nblk),
        in_specs=[col(COL_Q), col(COL_K), col(COL_V),
                  pl.BlockSpec((tb, LANES), lambda b, i: (i, 0)),
                  pl.BlockSpec((tb, LANES), lambda b, i: (i, 0)),
                  pl.BlockSpec((1, LANES), lambda b, i: (0, 0)),
                  pl.BlockSpec((1, LANES), lambda b, i: (0, 0))],
        out_specs=[out_tok, out_tok, out_tok,
                   pl.BlockSpec((1, 1, D_MODEL, tb), lambda b, i: (b, i, 0, 0)),
                   pl.BlockSpec((1, ATT_HEADS, nblk, tb), lambda b, i: (b, 0, 0, i))],
        out_shape=[jax.ShapeDtypeStruct((nb, s, D_MODEL), BF16),
                   jax.ShapeDtypeStruct((nb, s, D_MODEL), F32),
                   jax.ShapeDtypeStruct((nb, s, D_MODEL), BF16),
                   jax.ShapeDtypeStruct((nb, nblk, D_MODEL, tb), BF16),
                   jax.ShapeDtypeStruct((nb, ATT_HEADS, nblk, s), F32)],
        scratch_shapes=[pltpu.VMEM((nblk, D_MODEL), F32), pltpu.VMEM((tb, D_MODEL), F32)],
        compiler_params=_params(("parallel", "arbitrary")),
        name="qk",
    )(p3, p3, p3, cos, sin, qw, kw)


def _attn_kernel(q_ref, k_ref, vt_ref, sel_ref, o_ref):
    tb = MOBA_BLOCK
    hd = ATT_HEADDIM
    i = pl.program_id(2)
    nh = LANES // hd
    q = q_ref[0]
    lane_head = _iota((tb, LANES), 1) // hd
    causal = _iota((tb, tb), 0) <= _iota((tb, tb), 1)
    qhs = [jnp.where(lane_head == hh, q, jnp.zeros_like(q)) for hh in range(nh)]

    kd = k_ref[0, pl.ds(pl.multiple_of(i * tb, tb), tb), :]
    state = []
    for hh in range(nh):
        s = jnp.where(causal, _dot_nt(kd, qhs[hh]), NEG_INF)
        m = jnp.max(s, axis=0, keepdims=True)
        p = jnp.exp(s - m)
        state += [m, jnp.sum(p, axis=0, keepdims=True),
                  _dot(vt_ref[0, i, hh * hd:(hh + 1) * hd, :], p.astype(BF16))]

    n_trips = (i + 1) // 2

    def scores(jj):
        kj = k_ref[0, pl.ds(pl.multiple_of(jj * (2 * tb), 2 * tb), 2 * tb), :]
        return [_dot_nt(kj, qhs[hh]) for hh in range(nh)]

    def body(jj, carry):
        j = 2 * jj
        ss = carry[3 * nh:]
        nxt = scores(jnp.minimum(jj + 1, jnp.maximum(n_trips - 1, 0)))
        new = []
        for hh in range(nh):
            m, l, acc = carry[3 * hh:3 * hh + 3]
            for u in range(2):
                on = sel_ref[0, hh, pl.ds(j + u, 1), :] > 0.5
                s = ss[hh][u * tb:(u + 1) * tb]
                m_new = jnp.maximum(m, jnp.where(on, jnp.max(s, axis=0, keepdims=True), NEG_INF))
                p = jnp.exp(s - m_new)
                alpha = jnp.exp(m - m_new)
                pv = _dot(vt_ref[0, j + u, hh * hd:(hh + 1) * hd, :], p.astype(BF16))
                l = alpha * l + jnp.where(on, jnp.sum(p, axis=0, keepdims=True), 0.0)
                acc = acc * alpha + jnp.where(on, pv, 0.0)
                m = m_new
            new += [m, l, acc]
        return tuple(new + nxt)

    state = lax.fori_loop(0, n_trips, body, tuple(state + scores(0)))
    outs = [state[3 * hh + 2] / state[3 * hh + 1] for hh in range(nh)]
    o_ref[0] = jnp.concatenate(outs, axis=0).T.astype(BF16)


def _attn(qa, ka, vt, sel):
    nb, s, _ = qa.shape
    tb = MOBA_BLOCK
    nblk = s // tb
    npair = D_MODEL // LANES
    hp = LANES // ATT_HEADDIM
    return pl.pallas_call(
        _attn_kernel,
        grid=(nb, npair, nblk),
        in_specs=[pl.BlockSpec((1, tb, LANES), lambda b, p, i: (b, i, p)),
                  pl.BlockSpec((1, s, LANES), lambda b, p, i: (b, 0, p)),
                  pl.BlockSpec((1, nblk, LANES, tb), lambda b, p, i: (b, 0, p, 0)),
                  pl.BlockSpec((1, hp, nblk, tb), lambda b, p, i: (b, p, 0, i))],
        out_specs=pl.BlockSpec((1, tb, LANES), lambda b, p, i: (b, i, p)),
        out_shape=jax.ShapeDtypeStruct((nb, s, D_MODEL), BF16),
        compiler_params=_params(("parallel", "parallel", "arbitrary")),
        name="attn",
    )(qa, ka, vt, sel)


def _qk_s_kernel(q_ref, k_ref, cos_ref, sin_ref, qw_ref, kw_ref, qr_ref, kr_ref):
    rows = q_ref.shape[0]
    cos = cos_ref[...]
    sin = sin_ref[...]
    bd, first = _head_consts(rows)
    for cb in range(D_MODEL // LANES):
        sl = slice(cb * LANES, (cb + 1) * LANES)
        qr_ref[:, sl] = _head_norm_rope(q_ref[:, sl], qw_ref[...], cos, sin, bd, first)
        kr_ref[:, sl] = _head_norm_rope(k_ref[:, sl], kw_ref[...], cos, sin, bd, first)


def _qk_s(p2, cos, sin, qw, kw):
    n = p2.shape[0]
    col = lambda cidx: pl.BlockSpec((n, D_MODEL), lambda i: (0, cidx))
    full = lambda r: pl.BlockSpec((r, LANES), lambda i: (0, 0))
    return pl.pallas_call(
        _qk_s_kernel,
        grid=(1,),
        in_specs=[col(COL_Q), col(COL_K), full(n), full(n), full(1), full(1)],
        out_specs=[pl.BlockSpec((n, D_MODEL), lambda i: (0, 0))] * 2,
        out_shape=[jax.ShapeDtypeStruct((n, D_MODEL), F32)] * 2,
        compiler_params=_params(("arbitrary",)),
        name="qk_s",
    )(p2, p2, cos, sin, qw, kw)


def _dec_gate_kernel(pt_ref, q_ref, *refs, nblk, bps):
    del pt_ref
    k_refs = refs[:2 * bps]
    s_ref, sel_ref, qb_scr, gate_scr = refs[2 * bps:]
    j = pl.program_id(1)
    lane = _iota((ATT_HEADS, LANES), 1)

    @pl.when(j == 0)
    def _():
        qb_scr[...] = jnp.broadcast_to(q_ref[0], qb_scr.shape)
        gate_scr[...] = jnp.full_like(gate_scr, NEG_INF)

    for u in range(bps):
        for h in range(ATT_HEADS):
            qh = qb_scr[h]
            s_ref[0, u, h:h + 1, 0:PAGE_SIZE] = jnp.sum(k_refs[2 * u][h] * qh, axis=0, keepdims=True)
            s_ref[0, u, h:h + 1, PAGE_SIZE:] = jnp.sum(k_refs[2 * u + 1][h] * qh, axis=0, keepdims=True)
        s = s_ref[0, u]
        g = jnp.sum(s, axis=-1, keepdims=True) * (1.0 / MOBA_BLOCK)
        gate_scr[...] = jnp.where(lane == j * bps + u, g, gate_scr[...])
        s_ref[0, u] = s * (ATT_HEADDIM ** -0.5)

    @pl.when(j == pl.num_programs(1) - 1)
    def _():
        g = gate_scr[...]
        chosen = jnp.zeros((ATT_HEADS, LANES), F32)
        for _ in range(MOBA_TOPK):
            m = jnp.max(g, axis=-1, keepdims=True)
            idx = jnp.min(jnp.where(g == m, lane, LANES), axis=-1, keepdims=True)
            pick = lane == idx
            chosen = jnp.where(pick & (lane < nblk), 1.0, chosen)
            g = jnp.where(pick, NEG_INF, g)
        sel_ref[0] = chosen


def _dec_attend_kernel(pt_ref, blist_ref, nsel_ref, sel_ref, q_ref, kn_ref, vn_ref, *refs, bps):
    del pt_ref
    s_refs = refs[:bps]
    v_refs = refs[bps:3 * bps]
    o_ref, m_scr, l_scr, acc_scr = refs[3 * bps:]
    b = pl.program_id(0)
    j = pl.program_id(1)

    @pl.when(j == 0)
    def _():
        for h in range(ATT_HEADS):
            m_scr[h:h + 1, :] = jnp.sum(q_ref[0, h] * kn_ref[0, h], axis=0, keepdims=True) * (ATT_HEADDIM ** -0.5)
        l_scr[...] = jnp.ones_like(l_scr)
        acc_scr[...] = vn_ref[0]

    for u in range(bps):
        k = j * bps + u

        @pl.when(k < nsel_ref[b])
        def _(u=u, k=k):
            lane = _iota((ATT_HEADS, LANES), 1)
            on = jnp.sum(jnp.where(lane == blist_ref[b, k], sel_ref[0], 0.0), axis=-1, keepdims=True) > 0.5
            s = jnp.where(on, s_refs[u][0, 0], NEG_INF)
            m = m_scr[...]
            m_new = jnp.maximum(m, jnp.max(s, axis=-1, keepdims=True))
            p = jnp.exp(s - m_new)
            alpha = jnp.exp(m - m_new)
            m_scr[...] = m_new
            l_scr[...] = alpha * l_scr[...] + jnp.sum(p, axis=-1, keepdims=True)
            v0_ref, v1_ref = v_refs[2 * u], v_refs[2 * u + 1]
            for h in range(ATT_HEADS):
                pv = (jnp.sum(v0_ref[h] * p[h:h + 1, 0:PAGE_SIZE], axis=-1, keepdims=True)
                      + jnp.sum(v1_ref[h] * p[h:h + 1, PAGE_SIZE:], axis=-1, keepdims=True))
                acc_scr[h] = alpha[h:h + 1, :] * acc_scr[h] + pv

    @pl.when(j == pl.num_programs(1) - 1)
    def _():
        for h in range(ATT_HEADS):
            o_ref[0, h] = acc_scr[h] / l_scr[h:h + 1, :]


def _dec_attn(q_rot, k_new, v_new, pool_k, pool_v, page_table):
    nb = q_rot.shape[0]
    n_pages = page_table.shape[1]
    per = MOBA_BLOCK // PAGE_SIZE
    assert per == 2
    nblk = n_pages // per
    assert MOBA_TOPK <= nblk <= LANES
    col_shape = (1, ATT_HEADS, ATT_HEADDIM, 1)
    page_shape = (None, ATT_HEADS, ATT_HEADDIM, PAGE_SIZE)
    s_shape = (1, 1, ATT_HEADS, MOBA_BLOCK)

    bps = DEC_BLOCKS_PER_STEP
    assert nblk % bps == 0
    k_page = lambda r: pl.BlockSpec(page_shape, lambda b, j, pt: (pt[b, per * bps * j + r], 0, 0, 0))
    gate_spec = pltpu.PrefetchScalarGridSpec(
        num_scalar_prefetch=1,
        grid=(nb, nblk // bps),
        in_specs=[pl.BlockSpec(col_shape, lambda b, j, pt: (b, 0, 0, 0))] + [k_page(r) for r in range(per * bps)],
        out_specs=[pl.BlockSpec((1, bps, ATT_HEADS, MOBA_BLOCK), lambda b, j, pt: (b, j, 0, 0)),
                   pl.BlockSpec((1, ATT_HEADS, LANES), lambda b, j, pt: (b, 0, 0))],
        scratch_shapes=[pltpu.VMEM((ATT_HEADS, ATT_HEADDIM, PAGE_SIZE), F32), pltpu.VMEM((ATT_HEADS, LANES), F32)])
    scores, sel = pl.pallas_call(
        functools.partial(_dec_gate_kernel, nblk=nblk, bps=bps),
        grid_spec=gate_spec,
        out_shape=[jax.ShapeDtypeStruct((nb, nblk, ATT_HEADS, MOBA_BLOCK), F32),
                   jax.ShapeDtypeStruct((nb, ATT_HEADS, LANES), F32)],
        compiler_params=_params(("parallel", "arbitrary")),
        name="dec_gate",
    )(page_table, q_rot, *([pool_k] * (per * bps)))

    flag = (jnp.max(sel[:, :, :nblk], axis=1) > 0.5).astype(I32)
    nsel = jnp.sum(flag, axis=1)
    abps = DEC_ATTEND_BLOCKS_PER_STEP
    nsteps = -(-min(nblk, ATT_HEADS * MOBA_TOPK) // abps)
    order = jnp.argsort(1 - flag, axis=1, stable=True).astype(I32)
    slot = jnp.minimum(jnp.arange(nsteps * abps, dtype=I32)[None, :], (nsel - 1)[:, None])
    blist = jnp.take_along_axis(order, slot, axis=1)

    col = pl.BlockSpec(col_shape, lambda b, j, pt, bl, ns: (b, 0, 0, 0))
    s_blk = lambda u: pl.BlockSpec(s_shape, lambda b, j, pt, bl, ns: (b, bl[b, j * abps + u], 0, 0))
    v_page = lambda u, half: pl.BlockSpec(
        page_shape, lambda b, j, pt, bl, ns: (pt[b, per * bl[b, j * abps + u] + half], 0, 0, 0))

    attend_spec = pltpu.PrefetchScalarGridSpec(
        num_scalar_prefetch=3,
        grid=(nb, nsteps),
        in_specs=([pl.BlockSpec((1, ATT_HEADS, LANES), lambda b, j, pt, bl, ns: (b, 0, 0)), col, col, col]
                  + [s_blk(u) for u in range(abps)]
                  + [v_page(u, half) for u in range(abps) for half in range(per)]),
        out_specs=col,
        scratch_shapes=[pltpu.VMEM((ATT_HEADS, 1), F32), pltpu.VMEM((ATT_HEADS, 1), F32),
                        pltpu.VMEM((ATT_HEADS, ATT_HEADDIM, 1), F32)])
    return pl.pallas_call(
        functools.partial(_dec_attend_kernel, bps=abps),
        grid_spec=attend_spec,
        out_shape=jax.ShapeDtypeStruct((nb, ATT_HEADS, ATT_HEADDIM, 1), F32),
        compiler_params=_params(("parallel", "arbitrary")),
        name="dec_attend",
    )(page_table, blist, nsel, sel, q_rot, k_new, v_new, *([scores] * abps), *([pool_v] * (abps * per)))


def _headnorm_kernel(x_ref, w_ref, o_ref):
    for h in range(MEM_HEADS):
        sl = slice(h * MEM_HEADDIM, (h + 1) * MEM_HEADDIM)
        x = x_ref[:, sl]
        ms = jnp.mean(x * x, axis=-1, keepdims=True)
        o_ref[:, sl] = x * lax.rsqrt(ms + EPS) * w_ref[...]


def _headnorm(x2, w, tm):
    t = x2.shape[0]
    return pl.pallas_call(
        _headnorm_kernel,
        grid=(t // tm,),
        in_specs=[pl.BlockSpec((tm, D_MODEL), lambda i: (i, 0)), pl.BlockSpec((1, MEM_HEADDIM), lambda i: (0, 0))],
        out_specs=pl.BlockSpec((tm, D_MODEL), lambda i: (i, 0)),
        out_shape=jax.ShapeDtypeStruct((t, D_MODEL), F32),
        compiler_params=_params(("parallel",)),
        name="headnorm",
    )(x2, w)


def _mem_kernel(q_ref, k_ref, v_ref, qw_ref, o_ref):
    kb = k_ref[0].astype(BF16)
    vb = v_ref[0].astype(BF16)
    for h in range(MEM_HEADS):
        sl = slice(h * MEM_HEADDIM, (h + 1) * MEM_HEADDIM)
        q = q_ref[0, :, sl]
        ms = jnp.mean(q * q, axis=-1, keepdims=True)
        qn = (q * lax.rsqrt(ms + EPS) * qw_ref[...]).astype(BF16)
        s = _dot_nt(qn, kb[:, sl]) * (MEM_HEADDIM ** -0.5)
        e = jnp.exp(s - jnp.max(s, axis=-1, keepdims=True))
        p = e / jnp.sum(e, axis=-1, keepdims=True)
        o_ref[0, :, sl] = _dot(p.astype(BF16), vb[:, sl]).astype(BF16)


def _mem_attend(q3, qcol, mk, mv, qw, tq):
    nb, s, _ = q3.shape
    m = mk.shape[1]
    return pl.pallas_call(
        _mem_kernel,
        grid=(nb, s // tq),
        in_specs=[pl.BlockSpec((1, tq, D_MODEL), lambda b, i: (b, i, qcol)),
                  pl.BlockSpec((1, m, D_MODEL), lambda b, i: (b, 0, 0)),
                  pl.BlockSpec((1, m, D_MODEL), lambda b, i: (b, 0, 0)),
                  pl.BlockSpec((1, MEM_HEADDIM), lambda b, i: (0, 0))],
        out_specs=pl.BlockSpec((1, tq, D_MODEL), lambda b, i: (b, i, 0)),
        out_shape=jax.ShapeDtypeStruct((nb, s, D_MODEL), BF16),
        compiler_params=_params(("parallel", "parallel")),
        name="mem_attend",
    )(q3, mk, mv, qw)


def _merge_kernel(ys_ref, oa_ref, om_ref, g1_ref, g2_ref, g3_ref, x_ref, ws_ref, wa_ref, wm_ref, wo_ref, nw_ref,
                  x1_ref, hf_ref):
    merged = (_sigmoid(g1_ref[...]) * _dot(ys_ref[...], ws_ref[...])
              + _sigmoid(g2_ref[...]) * _dot(oa_ref[...], wa_ref[...])
              + _sigmoid(g3_ref[...]) * _dot(om_ref[...], wm_ref[...]))
    x1 = x_ref[...] + _dot(merged.astype(BF16), wo_ref[...])
    x1_ref[...] = x1
    ms = jnp.mean(x1 * x1, axis=-1, keepdims=True)
    hf_ref[...] = (x1 * lax.rsqrt(ms + EPS) * nw_ref[...]).astype(BF16)


def _merge(y_ssd, o_att, o_mem, p2, x2, ws, wa, wm, wo, nw, tm):
    t = x2.shape[0]
    tok = lambda w, c=0: pl.BlockSpec((tm, w), lambda i: (i, c))
    res = lambda a: pl.BlockSpec(a.shape, lambda i: (0, 0))
    return pl.pallas_call(
        _merge_kernel,
        grid=(t // tm,),
        in_specs=[tok(SSD_INNER), tok(D_MODEL), tok(D_MODEL),
                  tok(D_MODEL, COL_G), tok(D_MODEL, COL_G + 1), tok(D_MODEL, COL_G + 2), tok(D_MODEL),
                  res(ws), res(wa), res(wm), res(wo), res(nw)],
        out_specs=[tok(D_MODEL), tok(D_MODEL)],
        out_shape=[jax.ShapeDtypeStruct((t, D_MODEL), F32), jax.ShapeDtypeStruct((t, D_MODEL), BF16)],
        compiler_params=_params(("parallel",)),
        name="merge",
    )(y_ssd, o_att, o_mem, p2, p2, p2, x2, ws, wa, wm, wo, nw)


def _top16(src_scr, n_rows, tp):
    r_id = _iota((n_rows, tp), 0)
    slot = _iota((PEER_TOPK, tp), 0)

    def body(r, carry):
        vals, idxs = carry
        sc = src_scr[...]
        m = jnp.max(sc, axis=0, keepdims=True)
        idx = jnp.min(jnp.where(sc == m, r_id, n_rows), axis=0, keepdims=True)
        src_scr[...] = jnp.where(r_id == idx, NEG_INF, sc)
        return jnp.where(slot == r, m, vals), jnp.where(slot == r, idx, idxs)

    return lax.fori_loop(0, PEER_TOPK, body, (jnp.zeros((PEER_TOPK, tp), F32), jnp.zeros((PEER_TOPK, tp), I32)))


def _peer_sel_kernel(hf_ref, wqt_ref, keys_ref, a_ref, b_ref, g_ref,
                     qt_scr, sc_scr, sv_scr, si_scr, cand_scr, oa_scr, ob_scr, og_scr):
    tp = hf_ref.shape[0]
    nk = PEER_NKEYS
    kk = PEER_TOPK
    qt_scr[...] = _dot_nt(wqt_ref[...], hf_ref[...])

    def half(hc, carry):
        q_hc = qt_scr[pl.ds(pl.multiple_of(hc * nk, nk), nk), :]
        sc_scr[...] = _dot3(keys_ref[hc], q_hc, _dot)
        vals, idxs = _top16(sc_scr, nk, tp)
        sv_scr[hc] = vals
        si_scr[hc] = idxs
        return carry

    lax.fori_loop(0, 2 * PEER_HEADS, half, 0)

    def head(h, carry):
        sv1 = sv_scr[2 * h]
        sv2 = sv_scr[2 * h + 1]
        si1 = si_scr[2 * h]
        si2 = si_scr[2 * h + 1]
        cand_scr[0:kk, :] = sv1[0:1, :] + sv2
        b8 = _iota((SUBLANES, tp), 0)
        for a in range(1, SUBLANES):
            c = sv1[a:a + 1, :] + sv2[0:SUBLANES, :]
            n_b = kk // (a + 1)
            cand_scr[SUBLANES * (a + 1):SUBLANES * (a + 2), :] = c if n_b >= SUBLANES else jnp.where(b8 < n_b, c, NEG_INF)
        cand_scr[PEER_NCAND - SUBLANES:, :] = sv1[SUBLANES:, :] + sv2[0:1, :]
        fv, fr = _top16(cand_scr, PEER_NCAND, tp)
        mid = fr - kk
        fa = jnp.where(fr < kk, 0, jnp.where(fr < PEER_NCAND - SUBLANES, 1 + (mid >> 3), fr - (PEER_NCAND - 2 * SUBLANES)))
        fb = jnp.where(fr < kk, fr, jnp.where(fr < PEER_NCAND - SUBLANES, mid & (SUBLANES - 1), 0))
        e1 = jnp.zeros((kk, tp), I32)
        e2 = jnp.zeros((kk, tp), I32)
        for a in range(kk):
            e1 = e1 + jnp.where(fa == a, si1[a:a + 1, :], 0)
            e2 = e2 + jnp.where(fb == a, si2[a:a + 1, :], 0)
        e = jnp.exp(fv - fv[0:1, :])
        rows = pl.ds(pl.multiple_of(h * kk, kk), kk)
        oa_scr[rows, :] = e1.astype(F32)
        ob_scr[rows, :] = e2.astype(F32)
        og_scr[rows, :] = e / jnp.sum(e, axis=0, keepdims=True)
        return carry

    lax.fori_loop(0, PEER_HEADS, head, 0)
    a_ref[...] = oa_scr[...].T
    b_ref[...] = ob_scr[...].T
    g_ref[...] = og_scr[...].T


def _peer_sel(hf, wqt, keys, tp):
    t = hf.shape[0]
    kk = PEER_TOPK
    npick = PEER_HEADS * kk
    out = pl.BlockSpec((tp, npick), lambda i: (i, 0))
    return pl.pallas_call(
        _peer_sel_kernel,
        grid=(t // tp,),
        in_specs=[pl.BlockSpec((tp, D_MODEL), lambda i: (i, 0)),
                  pl.BlockSpec(wqt.shape, lambda i: (0, 0)),
                  pl.BlockSpec(keys.shape, lambda i: (0, 0, 0))],
        out_specs=[out, out, out],
        out_shape=[jax.ShapeDtypeStruct((t, npick), F32)] * 3,
        scratch_shapes=[pltpu.VMEM((wqt.shape[0], tp), F32),
                        pltpu.VMEM((PEER_NKEYS, tp), F32),
                        pltpu.VMEM((2 * PEER_HEADS, kk, tp), F32),
                        pltpu.VMEM((2 * PEER_HEADS, kk, tp), I32),
                        pltpu.VMEM((PEER_NCAND, tp), F32),
                        pltpu.VMEM((npick, tp), F32),
                        pltpu.VMEM((npick, tp), F32),
                        pltpu.VMEM((npick, tp), F32)],
        compiler_params=_params(("parallel",)),
        name="peer_sel",
    )(hf, wqt, keys)


def _peer_kernel(hf_ref, a_ref, b_ref, g_ref, x1_ref, u_ref, v_ref, o_ref, g_scr, *, tm, te):
    nk = PEER_NKEYS
    e = pl.program_id(1)
    n1 = te // nk

    @pl.when(e == 0)
    def _():
        o_ref[...] = x1_ref[...]
        sub = _iota((nk, LANES), 0).astype(F32)

        def toks(tg, carry):
            for u in range(PEER_BUILD_TOKENS):
                t = tg * PEER_BUILD_TOKENS + u
                row = pl.ds(t, 1)
                y = jnp.where(sub == a_ref[row, :], g_ref[row, :], 0.0).astype(BF16)
                z = jnp.where(sub == b_ref[row, :], 1.0, 0.0).astype(BF16)
                g_scr[pl.ds(pl.multiple_of(t * nk, nk), nk), :] = _dot_nt(y, z)
            return carry

        lax.fori_loop(0, tm // PEER_BUILD_TOKENS, toks, 0)

    hf = hf_ref[...]
    half = te // 2
    for c in range(2):
        act = _dot_nt(hf, u_ref[c * half:(c + 1) * half, :])
        act = 0.5 * act * (1.0 + lax.erf(act * (2.0 ** -0.5)))
        ws = []
        for k in range(half // nk):
            gk = g_scr[pl.ds(e * n1 + c * (half // nk) + k, tm, stride=nk), :]
            ws.append((act[:, k * nk:(k + 1) * nk] * gk).astype(BF16))
        o_ref[...] += _dot(jnp.concatenate(ws, axis=1), v_ref[c * half:(c + 1) * half, :])


def _peer(hf, a, b, g, x1, u, v, tm, te):
    t = hf.shape[0]
    n_exp = u.shape[0]
    npick = a.shape[1]
    tok = lambda w: pl.BlockSpec((tm, w), lambda i, e: (i, 0))
    exp = pl.BlockSpec((te, D_MODEL), lambda i, e: (e, 0))
    return pl.pallas_call(
        functools.partial(_peer_kernel, tm=tm, te=te),
        grid=(t // tm, n_exp // te),
        in_specs=[tok(D_MODEL), tok(npick), tok(npick), tok(npick), tok(D_MODEL), exp, exp],
        out_specs=tok(D_MODEL),
        out_shape=jax.ShapeDtypeStruct((t, D_MODEL), F32),
        scratch_shapes=[pltpu.VMEM((tm * PEER_NKEYS, PEER_NKEYS), F32)],
        compiler_params=_params(("parallel", "arbitrary")),
        name="peer",
    )(hf, a, b, g, x1, u, v)


def _pack_w_in(w_in):
    n0 = SSD_INNER + SSD_CONV_DIM
    w_main = jnp.concatenate([w_in[:, :n0], w_in[:, n0 + SSD_HEADS:]], axis=1).astype(BF16)
    w_dt = jnp.pad(w_in[:, n0:n0 + SSD_HEADS], ((0, 0), (0, LANES - SSD_HEADS))).astype(BF16)
    return w_main, w_dt


def _pad_lanes(v):
    return jnp.pad(v, (0, LANES - v.shape[0]))[None]


def kernel(x_prompt, x_sample, mem_prompt, cache_k, cache_v, page_table, cache_mem_k, cache_mem_v, state_ssm, state_conv, norm1_w, w_in, conv_w, conv_b, dt_bias, a_log, d_skip, ssd_norm_w, q_norm_w, k_norm_w, mem_norm_w, w_mem_k, w_mem_v, mem_qn_w, mem_kn_w, w_br_ssd, w_br_attn, w_br_mem, w_out, norm2_w, peer_wq, peer_keys, peer_u, peer_v):
    bp, sp, _ = x_prompt.shape
    bs, ss, _ = x_sample.shape
    assert ss == 1 and cache_k.shape[0] == 1
    tp, ts = bp * sp, bs * ss

    w_main, w_dt = _pack_w_in(w_in[0])
    nw1 = norm1_w[0][None]
    ssd_w = (conv_w[0], conv_b[0][None], _pad_lanes(dt_bias[0]), _pad_lanes(a_log[0]),
             jnp.repeat(d_skip[0], SSD_HEADDIM)[None], ssd_norm_w[0][None])
    qw2 = jnp.tile(q_norm_w[0], LANES // ATT_HEADDIM)[None]
    kw2 = jnp.tile(k_norm_w[0], LANES // ATT_HEADDIM)[None]
    w_mem_kv = jnp.concatenate([w_mem_k[0], w_mem_v[0]], axis=1).astype(BF16)
    merge_w = (w_br_ssd[0].astype(BF16), w_br_attn[0].astype(BF16), w_br_mem[0].astype(BF16),
               w_out[0].astype(BF16), norm2_w[0][None])
    wqt = peer_wq[0].T.astype(BF16)
    keys = peer_keys[0].reshape(2 * PEER_HEADS, PEER_NKEYS, -1)
    u_bf = peer_u[0].astype(BF16)
    v_bf = peer_v[0].astype(BF16)
    mem_qw = mem_qn_w[0][None]

    xp = x_prompt.reshape(tp, D_MODEL)
    p_main = _proj(xp, nw1, w_main, 1024, 1024)
    p_dt = _proj(xp, nw1, w_dt, 1024, LANES)
    p3 = p_main.reshape(bp, sp, N_MAIN)
    y_ssd, ssm_p = _ssd(p3, p_dt.reshape(bp, sp, LANES), jnp.zeros((bp, SUBLANES, SSD_CONV_DIM), F32),
                        jnp.zeros((bp, SSD_HEADS, SSD_HEADDIM, SSD_STATE), F32), *ssd_w, SSD_CHUNK)
    cos, sin = _rope_tables(jnp.arange(sp, dtype=I32))
    qa, k_rot, ka, vt, sel = _qk(p3, cos, sin, qw2, kw2)
    o_att = _attn(qa, ka, vt, sel)
    n_mem = mem_prompt.shape[1]
    kv = _proj(mem_prompt.reshape(bp * n_mem, D_MODEL), mem_norm_w[0][None], w_mem_kv, 256, D_MODEL)
    mem_k = _headnorm(kv, mem_kn_w[0][None], 256)
    mem_v = kv[:, D_MODEL:]
    o_mem = _mem_attend(p3, COL_MQ, mem_k.reshape(bp, n_mem, D_MODEL), mem_v.reshape(bp, n_mem, D_MODEL), mem_qw, 512)
    x1, hf = _merge(y_ssd.reshape(tp, SSD_INNER), o_att.reshape(tp, D_MODEL), o_mem.reshape(tp, D_MODEL),
                    p_main, xp, *merge_w, 512)
    pa, pb, pg = _peer_sel(hf, wqt, keys, 256)
    y_p = _peer(hf, pa, pb, pg, x1, u_bf, v_bf, 256, 1024)

    xs = x_sample.reshape(ts, D_MODEL)
    ps_main = _proj(xs, nw1, w_main, ts, 1024)
    ps_dt = _proj(xs, nw1, w_dt, ts, LANES)
    ps3 = ps_main.reshape(bs, 1, N_MAIN)
    conv_prev8 = jnp.pad(state_conv[0], ((0, 0), (SUBLANES - (SSD_CONV - 1), 0), (0, 0)))
    ys_ssd, ssm_s = _ssd(ps3, ps_dt.reshape(bs, 1, LANES), conv_prev8, state_ssm[0], *ssd_w, 1)
    cos_s, sin_s = _rope_tables(jnp.full((ts,), PAST_LEN, I32))
    q_rot_s, k_rot_s = _qk_s(ps_main, cos_s, sin_s, qw2, kw2)
    v_s = ps_main[:, COL_V * D_MODEL:(COL_V + 1) * D_MODEL]
    cols = lambda v: v.reshape(bs, ATT_HEADS, ATT_HEADDIM, 1)
    pool = lambda c: jnp.transpose(c[0], (0, 2, 3, 1))
    o_att_s = _dec_attn(cols(q_rot_s), cols(k_rot_s), cols(v_s), pool(cache_k), pool(cache_v), page_table)
    mq8 = jnp.broadcast_to(ps3[:, :, COL_MQ * D_MODEL:(COL_MQ + 1) * D_MODEL], (bs, SUBLANES, D_MODEL))
    o_mem_s = _mem_attend(mq8, 0, cache_mem_k[0].reshape(bs, n_mem, D_MODEL),
                          cache_mem_v[0].reshape(bs, n_mem, D_MODEL), mem_qw, SUBLANES)[:, 0]
    x1_s, hf_s = _merge(ys_ssd.reshape(ts, SSD_INNER), o_att_s.reshape(ts, D_MODEL).astype(BF16), o_mem_s,
                        ps_main, xs, *merge_w, ts)
    pad_rows = lambda v: jnp.pad(v, ((0, LANES - ts), (0, 0)))
    hf_sp = pad_rows(hf_s)
    sa, sb, sg = _peer_sel(hf_sp, wqt, keys, LANES)
    y_s = _peer(hf_sp, sa, sb, sg, pad_rows(x1_s), u_bf, v_bf, LANES, 1024)[:ts]

    kv_shape = (1, bp, sp, ATT_HEADS, ATT_HEADDIM)
    mem_shape = (1, bp, n_mem, MEM_HEADS, MEM_HEADDIM)
    xbc_cols = slice(COL_X * 1024, COL_X * 1024 + SSD_CONV_DIM)
    conv_p = p3[:, sp - (SSD_CONV - 1):, xbc_cols]
    conv_s = jnp.concatenate([state_conv[0][:, 1:], ps3[:, :, xbc_cols]], axis=1)
    return (y_p.reshape(bp, sp, D_MODEL), y_s.reshape(bs, ss, D_MODEL),
            k_rot.reshape(kv_shape), p3[:, :, COL_V * 1024:(COL_V + 1) * 1024].reshape(kv_shape),
            mem_k.reshape(mem_shape), mem_v.reshape(mem_shape),
            ssm_p[None], conv_p[None],
            k_rot_s.reshape(1, bs, ss, ATT_HEADS, ATT_HEADDIM), v_s.reshape(1, bs, ss, ATT_HEADS, ATT_HEADDIM),
            ssm_s[None], conv_s[None])
```

```python
import functools
import math

import jax
import jax.numpy as jnp
from jax import lax
from jax.experimental import pallas as pl
from jax.experimental.pallas import tpu as pltpu

F32 = jnp.float32
BF16 = jnp.bfloat16
I32 = jnp.int32
EPS = 1e-6
NEG_INF = float("-inf")

D_MODEL = 1024
SSD_INNER = 2048
SSD_HEADS = 32
SSD_HEADDIM = 64
SSD_GROUPS = 8
SSD_STATE = 128
SSD_CONV = 4
SSD_CHUNK = 128
SSD_CONV_DIM = 4096
ATT_HEADS = 16
ATT_HEADDIM = 64
MOBA_BLOCK = 256
MOBA_TOPK = 3
ROPE_THETA = 10000.0
MEM_HEADS = 4
MEM_HEADDIM = 256
PEER_HEADS = 8
PEER_NKEYS = 128
PEER_TOPK = 16
PEER_EXPERT_CHUNK = 512
PEER_BUILD_TOKENS = 16
PEER_NCAND = 80
PAST_LEN = 16384
PAGE_SIZE = 128
DEC_BLOCKS_PER_STEP = 4
DEC_ATTEND_BLOCKS_PER_STEP = 2

LANES = 128
SUBLANES = 8
VMEM_LIMIT = 56 * 1024 * 1024

COL_Z, COL_X, COL_BC, COL_Q, COL_K, COL_V, COL_MQ, COL_G = 0, 2, 4, 6, 7, 8, 9, 10
N_MAIN = 13 * 1024


def _params(sem):
    return pltpu.CompilerParams(dimension_semantics=sem, vmem_limit_bytes=VMEM_LIMIT)


def _dot(a, b):
    return jnp.dot(a, b, preferred_element_type=F32)


def _dot_nt(a, b):
    return lax.dot_general(a, b, (((1,), (1,)), ((), ())), preferred_element_type=F32)


def _dot_tn(a, b):
    return lax.dot_general(a, b, (((0,), (0,)), ((), ())), preferred_element_type=F32)


def _split(a):
    hi = a.astype(BF16)
    lo = (a - hi.astype(F32)).astype(BF16)
    return hi, lo


def _dot3(a, b, dot):
    ah, al = _split(a)
    bh, bl = _split(b)
    return dot(ah, bh) + (dot(ah, bl) + dot(al, bh))


def _sigmoid(x):
    return 1.0 / (1.0 + jnp.exp(-x))


def _iota(shape, axis):
    return lax.broadcasted_iota(I32, shape, axis)


def _tree(x, op):
    while x.shape[0] > 1:
        h = x.shape[0] // 2
        x = op(x[:h], x[h:])
    return x[0]


def _proj_kernel(x_ref, nw_ref, w_ref, o_ref, h_scr):
    @pl.when(pl.program_id(1) == 0)
    def _():
        x = x_ref[...]
        ms = jnp.mean(x * x, axis=-1, keepdims=True)
        h_scr[...] = (x * lax.rsqrt(ms + EPS) * nw_ref[...]).astype(BF16)

    o_ref[...] = _dot(h_scr[...], w_ref[...])


def _proj(x, nw, w, tm, tn):
    t, d = x.shape
    n = w.shape[1]
    return pl.pallas_call(
        _proj_kernel,
        grid=(t // tm, n // tn),
        in_specs=[pl.BlockSpec((tm, d), lambda i, j: (i, 0)),
                  pl.BlockSpec((1, d), lambda i, j: (0, 0)),
                  pl.BlockSpec((d, tn), lambda i, j: (0, j))],
        out_specs=pl.BlockSpec((tm, tn), lambda i, j: (i, j)),
        out_shape=jax.ShapeDtypeStruct((t, n), F32),
        scratch_shapes=[pltpu.VMEM((tm, d), BF16)],
        compiler_params=_params(("parallel", "arbitrary")),
        name="proj",
    )(x, nw, w)


def _ssd_kernel(z_ref, x_ref, bc_ref, dt_ref, cprev_ref, h0_ref, cw_ref, cb_ref, dtb_ref, alog_ref,
                dsk_ref, nw_ref, y_ref, hl_ref, buf, st, ycat, *, rows):
    L = SSD_CHUNK
    c = pl.program_id(1)

    @pl.when(c == 0)
    def _():
        buf[0:SUBLANES, :] = cprev_ref[0]
        st[...] = h0_ref[0].reshape(st.shape)

    if rows < L:
        buf[SUBLANES:, :] = jnp.zeros((L, SSD_CONV_DIM), F32)
    buf[SUBLANES:SUBLANES + rows, 0:SSD_INNER] = x_ref[0]
    buf[SUBLANES:SUBLANES + rows, SSD_INNER:] = bc_ref[0]

    acc = buf[SUBLANES:SUBLANES + L, :] * cw_ref[SSD_CONV - 1:SSD_CONV, :]
    for j in range(SSD_CONV - 1):
        off = SUBLANES - (SSD_CONV - 1) + j
        acc = acc + buf[off:off + L, :] * cw_ref[j:j + 1, :]
    acc = acc + cb_ref[...]
    xbc = acc * _sigmoid(acc)
    buf[0:SUBLANES, :] = buf[L:L + SUBLANES, :]

    xs = xbc[:, :SSD_INNER]
    n_gn = SSD_GROUPS * SSD_STATE
    bm = xbc[:, SSD_INNER:SSD_INNER + n_gn].astype(BF16)
    cm = xbc[:, SSD_INNER + n_gn:].astype(BF16)

    dt_raw = dt_ref[0] if rows == L else jnp.broadcast_to(dt_ref[0], (L, LANES))
    pre = dt_raw + dtb_ref[...]
    dt = jnp.maximum(pre, 0.0) + jnp.log1p(jnp.exp(-jnp.abs(pre)))
    row_id = _iota((L, LANES), 0)
    dt = jnp.where(row_id < rows, dt, 0.0)
    a = -jnp.exp(alog_ref[...])
    da = dt * a
    tri = _iota((L, L), 0) >= _iota((L, L), 1)
    a_cs = jnp.dot(tri.astype(F32), da, preferred_element_type=F32, precision=lax.Precision.HIGHEST)
    a_cs_t = a_cs.T
    a_last = a_cs[L - 1:L, :]
    e_cs = jnp.exp(a_cs)
    d_end = jnp.exp(a_last - a_cs)
    e_last = jnp.exp(a_last)

    head_lanes = (_iota((LANES, SSD_INNER), 1) // SSD_HEADDIM == _iota((LANES, SSD_INNER), 0))
    head_lanes = jnp.where(head_lanes, 1.0, 0.0).astype(BF16)

    def spread(v):
        p1 = v.astype(BF16)
        r1 = v - p1.astype(F32)
        p2 = r1.astype(BF16)
        p3 = (r1 - p2.astype(F32)).astype(BF16)
        return _dot(p1, head_lanes) + (_dot(p2, head_lanes) + _dot(p3, head_lanes))

    xdt_all = xs * spread(dt)
    xdt_bf = xdt_all.astype(BF16)
    xw_bf = (xdt_all * spread(d_end)).astype(BF16)
    e_cs_all = spread(e_cs)
    left = _iota((L, LANES), 1) < SSD_HEADDIM
    top = _iota((LANES, SSD_STATE), 0) < SSD_HEADDIM

    for g in range(SSD_GROUPS):
        bg = bm[:, g * SSD_STATE:(g + 1) * SSD_STATE]
        cg = cm[:, g * SSD_STATE:(g + 1) * SSD_STATE]
        cb = _dot_nt(cg, bg)
        for r in range(SSD_HEADS // SSD_GROUPS // 2):
            pair = g * (SSD_HEADS // SSD_GROUPS // 2) + r
            ps = slice(pair * LANES, (pair + 1) * LANES)
            ms_ = []
            for h in (2 * pair, 2 * pair + 1):
                decay = jnp.where(tri, jnp.exp(a_cs[:, h:h + 1] - a_cs_t[h:h + 1, :]), 0.0)
                ms_.append((cb * decay).astype(BF16))
            xp = xdt_bf[:, ps]
            yd = jnp.where(left, _dot(ms_[0], xp), _dot(ms_[1], xp))
            hp = st[pair]
            yo = _dot_nt(cg, hp.astype(BF16)) * e_cs_all[:, ps]
            keep = jnp.where(top, e_last[:, 2 * pair:2 * pair + 1], e_last[:, 2 * pair + 1:2 * pair + 2])
            st[pair] = hp * keep + _dot_tn(xw_bf[:, ps], bg)
            ycat[:, ps] = yd + yo + xs[:, ps] * dsk_ref[:, ps]

    z = z_ref[0]
    if rows < L:
        y = ycat[0:rows, :] * (z * _sigmoid(z))
    else:
        y = ycat[...] * (z * _sigmoid(z))
    ms = jnp.mean(y * y, axis=-1, keepdims=True)
    y_ref[0] = (y * lax.rsqrt(ms + EPS) * nw_ref[...]).astype(BF16)

    @pl.when(c == pl.num_programs(1) - 1)
    def _():
        hl_ref[0] = st[...].reshape(SSD_HEADS, SSD_HEADDIM, SSD_STATE)


def _ssd(p3, dt3, conv_prev8, h0, cw, cb, dtb, alog, dsk, nw, rows):
    nb, s, _ = p3.shape
    nc = s // rows
    w2 = SSD_INNER
    const = lambda shape: pl.BlockSpec(shape, lambda b, c: tuple(0 for _ in shape))
    return pl.pallas_call(
        functools.partial(_ssd_kernel, rows=rows),
        grid=(nb, nc),
        in_specs=[pl.BlockSpec((1, rows, w2), lambda b, c: (b, c, COL_Z // 2)),
                  pl.BlockSpec((1, rows, w2), lambda b, c: (b, c, COL_X // 2)),
                  pl.BlockSpec((1, rows, w2), lambda b, c: (b, c, COL_BC // 2)),
                  pl.BlockSpec((1, rows, LANES), lambda b, c: (b, c, 0)),
                  pl.BlockSpec((1, SUBLANES, SSD_CONV_DIM), lambda b, c: (b, 0, 0)),
                  pl.BlockSpec((1, SSD_HEADS, SSD_HEADDIM, SSD_STATE), lambda b, c: (b, 0, 0, 0)),
                  const((SSD_CONV, SSD_CONV_DIM)), const((1, SSD_CONV_DIM)), const((1, LANES)),
                  const((1, LANES)), const((1, SSD_INNER)), const((1, SSD_INNER))],
        out_specs=[pl.BlockSpec((1, rows, w2), lambda b, c: (b, c, 0)),
                   pl.BlockSpec((1, SSD_HEADS, SSD_HEADDIM, SSD_STATE), lambda b, c: (b, 0, 0, 0))],
        out_shape=[jax.ShapeDtypeStruct((nb, s, w2), BF16),
                   jax.ShapeDtypeStruct((nb, SSD_HEADS, SSD_HEADDIM, SSD_STATE), F32)],
        scratch_shapes=[pltpu.VMEM((SUBLANES + SSD_CHUNK, SSD_CONV_DIM), F32),
                        pltpu.VMEM((SSD_HEADS // 2, 2 * SSD_HEADDIM, SSD_STATE), F32),
                        pltpu.VMEM((SSD_CHUNK, SSD_INNER), F32)],
        compiler_params=_params(("parallel", "arbitrary")),
        name="ssd",
    )(p3, p3, p3, dt3, conv_prev8, h0, cw, cb, dtb, alog, dsk, nw)


def _rope_tables(pos):
    d = ATT_HEADDIM
    inv = 1.0 / (ROPE_THETA ** (jnp.arange(0, d, 2, dtype=F32) / d))
    ang = pos.astype(F32)[:, None] * inv[None, :]
    cos = jnp.cos(ang)
    sin = jnp.sin(ang)
    cos = jnp.concatenate([cos, cos, cos, cos], axis=-1)
    sin = jnp.concatenate([-sin, sin, -sin, sin], axis=-1)
    return cos, sin


def _head_norm_rope(x, w, cos, sin, bd, first):
    hi, lo = _split(x * x)
    ms = _dot(hi, bd) + _dot(lo, bd)
    xn = x * lax.rsqrt(ms + EPS) * w
    partner = jnp.where(first, pltpu.roll(xn, LANES - ATT_HEADDIM // 2, 1), pltpu.roll(xn, ATT_HEADDIM // 2, 1))
    return xn * cos + partner * sin


def _head_consts(rows):
    r = _iota((LANES, LANES), 0) // ATT_HEADDIM
    c = _iota((LANES, LANES), 1) // ATT_HEADDIM
    bd = jnp.where(r == c, 1.0 / ATT_HEADDIM, 0.0).astype(BF16)
    first = (_iota((rows, LANES), 1) % ATT_HEADDIM) < (ATT_HEADDIM // 2)
    return bd, first


def _qk_kernel(q_ref, k_ref, v_ref, cos_ref, sin_ref, qw_ref, kw_ref,
               qa_ref, kr_ref, ka_ref, vo_ref, vt_ref, sel_ref, km_scr, qr_scr, *, nblk):
    tb = MOBA_BLOCK
    i = pl.program_id(1)

    @pl.when(i == 0)
    def _():
        km_scr[...] = jnp.zeros_like(km_scr)

    cos = cos_ref[...]
    sin = sin_ref[...]
    bd, first = _head_consts(tb)
    for cb in range(D_MODEL // LANES):
        sl = slice(cb * LANES, (cb + 1) * LANES)
        qr = _head_norm_rope(q_ref[0, :, sl], qw_ref[...], cos, sin, bd, first)
        kr = _head_norm_rope(k_ref[0, :, sl], kw_ref[...], cos, sin, bd, first)
        qr_scr[:, sl] = qr
        qa_ref[0, :, sl] = (qr * (ATT_HEADDIM ** -0.5)).astype(BF16)
        kr_ref[0, :, sl] = kr
        ka_ref[0, :, sl] = kr.astype(BF16)
        k_mean = jnp.sum(kr, axis=0, keepdims=True) * (1.0 / tb)
        km_scr[:, sl] = jnp.where(_iota((nblk, LANES), 0) == i, k_mean, km_scr[:, sl])
    v = v_ref[0]
    vo_ref[0] = v
    vt_ref[0, 0] = v.T.astype(BF16)

    km = km_scr[...]
    lane_head = _iota((nblk, D_MODEL), 1) // ATT_HEADDIM
    kmt = jnp.concatenate([jnp.where(lane_head == h, km, 0.0) for h in range(ATT_HEADS)], axis=0)
    gate = _dot3(kmt, qr_scr[...], _dot_nt)
    n_id = _iota((nblk, tb), 0)
    valid = n_id < i
    for h in range(ATT_HEADS):
        g = jnp.where(valid, gate[h * nblk:(h + 1) * nblk, :], NEG_INF)
        chosen = jnp.zeros((nblk, tb), F32)
        for _ in range(MOBA_TOPK):
            m = jnp.max(g, axis=0, keepdims=True)
            idx = jnp.min(jnp.where(g == m, n_id, nblk), axis=0, keepdims=True)
            pick = n_id == idx
            chosen = jnp.where(pick & valid, 1.0, chosen)
            g = jnp.where(pick, NEG_INF, g)
        sel_ref[0, h] = chosen


def _qk(p3, cos, sin, qw, kw):
    nb, s, _ = p3.shape
    tb = MOBA_BLOCK
    nblk = s // tb
    col = lambda cidx: pl.BlockSpec((1, tb, D_MODEL), lambda b, i: (b, i, cidx))
    out_tok = pl.BlockSpec((1, tb, D_MODEL), lambda b, i: (b, i, 0))
    return pl.pallas_call(
        functools.partial(_qk_kernel, nblk=nblk),
        grid=(nb, nblk),
        in_specs=[col(COL_Q), col(COL_K), col(COL_V),
                  pl.BlockSpec((tb, LANES), lambda b, i: (i, 0)),
                  pl.BlockSpec((tb, LANES), lambda b, i: (i, 0)),
                  pl.BlockSpec((1, LANES), lambda b, i: (0, 0)),
                  pl.BlockSpec((1, LANES), lambda b, i: (0, 0))],
        out_specs=[out_tok, out_tok, out_tok, out_tok,
                   pl.BlockSpec((1, 1, D_MODEL, tb), lambda b, i: (b, i, 0, 0)),
                   pl.BlockSpec((1, ATT_HEADS, nblk, tb), lambda b, i: (b, 0, 0, i))],
        out_shape=[jax.ShapeDtypeStruct((nb, s, D_MODEL), BF16),
                   jax.ShapeDtypeStruct((nb, s, D_MODEL), F32),
                   jax.ShapeDtypeStruct((nb, s, D_MODEL), BF16),
                   jax.ShapeDtypeStruct((nb, s, D_MODEL), F32),
                   jax.ShapeDtypeStruct((nb, nblk, D_MODEL, tb), BF16),
                   jax.ShapeDtypeStruct((nb, ATT_HEADS, nblk, s), F32)],
        scratch_shapes=[pltpu.VMEM((nblk, D_MODEL), F32), pltpu.VMEM((tb, D_MODEL), F32)],
        compiler_params=_params(("parallel", "arbitrary")),
        name="qk",
    )(p3, p3, p3, cos, sin, qw, kw)


def _attn_kernel(q_ref, k_ref, vt_ref, sel_ref, o_ref):
    tb = MOBA_BLOCK
    hd = ATT_HEADDIM
    i = pl.program_id(2)
    nh = LANES // hd
    q = q_ref[0]
    lane_head = _iota((tb, LANES), 1) // hd
    causal = _iota((tb, tb), 0) <= _iota((tb, tb), 1)
    qhs = [jnp.where(lane_head == hh, q, jnp.zeros_like(q)) for hh in range(nh)]

    ones_rows = jnp.ones((2 * SUBLANES, tb), BF16)

    def v_ones(j, hh):
        return jnp.concatenate([vt_ref[0, j, hh * hd:(hh + 1) * hd, :], ones_rows], axis=0)

    kd = k_ref[0, pl.ds(pl.multiple_of(i * tb, tb), tb), :]
    state = []
    for hh in range(nh):
        s = jnp.where(causal, _dot_nt(kd, qhs[hh]), NEG_INF)
        m = jnp.max(s, axis=0, keepdims=True)
        p = jnp.exp(s - m)
        state += [m, _dot(v_ones(i, hh), p.astype(BF16))]

    n_trips = (i + 1) // 2

    def scores(jj):
        kj = k_ref[0, pl.ds(pl.multiple_of(jj * (2 * tb), 2 * tb), 2 * tb), :]
        return [_dot_nt(kj, qhs[hh]) for hh in range(nh)]

    def body(jj, carry):
        j = 2 * jj
        ss = carry[2 * nh:]
        nxt = scores(jnp.minimum(jj + 1, jnp.maximum(n_trips - 1, 0)))
        new = []
        for hh in range(nh):
            m, acc = carry[2 * hh:2 * hh + 2]
            for u in range(2):
                on = sel_ref[0, hh, pl.ds(j + u, 1), :] > 0.5
                s = ss[hh][u * tb:(u + 1) * tb]
                m_new = jnp.maximum(m, jnp.where(on, jnp.max(s, axis=0, keepdims=True), NEG_INF))
                p = jnp.exp(s - m_new)
                pv = _dot(v_ones(j + u, hh), p.astype(BF16))
                acc = acc * jnp.exp(m - m_new) + jnp.where(on, pv, 0.0)
                m = m_new
            new += [m, acc]
        return tuple(new + nxt)

    state = lax.fori_loop(0, n_trips, body, tuple(state + scores(0)))
    outs = [state[2 * hh + 1][:hd] / state[2 * hh + 1][hd:hd + 1] for hh in range(nh)]
    o_ref[0] = jnp.concatenate(outs, axis=0).T.astype(BF16)


def _attn(qa, ka, vt, sel):
    nb, s, _ = qa.shape
    tb = MOBA_BLOCK
    nblk = s // tb
    npair = D_MODEL // LANES
    hp = LANES // ATT_HEADDIM
    return pl.pallas_call(
        _attn_kernel,
        grid=(nb, npair, nblk),
        in_specs=[pl.BlockSpec((1, tb, LANES), lambda b, p, i: (b, i, p)),
                  pl.BlockSpec((1, s, LANES), lambda b, p, i: (b, 0, p)),
                  pl.BlockSpec((1, nblk, LANES, tb), lambda b, p, i: (b, 0, p, 0)),
                  pl.BlockSpec((1, hp, nblk, tb), lambda b, p, i: (b, p, 0, i))],
        out_specs=pl.BlockSpec((1, tb, LANES), lambda b, p, i: (b, i, p)),
        out_shape=jax.ShapeDtypeStruct((nb, s, D_MODEL), BF16),
        compiler_params=_params(("parallel", "parallel", "arbitrary")),
        name="attn",
    )(qa, ka, vt, sel)


def _qk_s_kernel(q_ref, k_ref, cos_ref, sin_ref, qw_ref, kw_ref, qr_ref, kr_ref):
    rows = q_ref.shape[0]
    cos = cos_ref[...]
    sin = sin_ref[...]
    bd, first = _head_consts(rows)
    for cb in range(D_MODEL // LANES):
        sl = slice(cb * LANES, (cb + 1) * LANES)
        qr_ref[:, sl] = _head_norm_rope(q_ref[:, sl], qw_ref[...], cos, sin, bd, first)
        kr_ref[:, sl] = _head_norm_rope(k_ref[:, sl], kw_ref[...], cos, sin, bd, first)


def _qk_s(p2, cos, sin, qw, kw):
    n = p2.shape[0]
    col = lambda cidx: pl.BlockSpec((n, D_MODEL), lambda i: (0, cidx))
    full = lambda r: pl.BlockSpec((r, LANES), lambda i: (0, 0))
    return pl.pallas_call(
        _qk_s_kernel,
        grid=(1,),
        in_specs=[col(COL_Q), col(COL_K), full(n), full(n), full(1), full(1)],
        out_specs=[pl.BlockSpec((n, D_MODEL), lambda i: (0, 0))] * 2,
        out_shape=[jax.ShapeDtypeStruct((n, D_MODEL), F32)] * 2,
        compiler_params=_params(("arbitrary",)),
        name="qk_s",
    )(p2, p2, cos, sin, qw, kw)


def _dec_gate_kernel(pt_ref, q_ref, *refs, nblk, bps):
    del pt_ref
    k_refs = refs[:2 * bps]
    s_ref, sel_ref, qb_scr, gate_scr = refs[2 * bps:]
    j = pl.program_id(1)
    lane = _iota((ATT_HEADS, LANES), 1)

    @pl.when(j == 0)
    def _():
        qb_scr[...] = jnp.broadcast_to(q_ref[0], qb_scr.shape)
        gate_scr[...] = jnp.full_like(gate_scr, NEG_INF)

    for u in range(bps):
        for h in range(ATT_HEADS):
            qh = qb_scr[h]
            s_ref[0, u, h:h + 1, 0:PAGE_SIZE] = jnp.sum(k_refs[2 * u][h] * qh, axis=0, keepdims=True)
            s_ref[0, u, h:h + 1, PAGE_SIZE:] = jnp.sum(k_refs[2 * u + 1][h] * qh, axis=0, keepdims=True)
        s = s_ref[0, u]
        g = jnp.sum(s, axis=-1, keepdims=True) * (1.0 / MOBA_BLOCK)
        gate_scr[...] = jnp.where(lane == j * bps + u, g, gate_scr[...])
        s_ref[0, u] = s * (ATT_HEADDIM ** -0.5)

    @pl.when(j == pl.num_programs(1) - 1)
    def _():
        g = gate_scr[...]
        chosen = jnp.zeros((ATT_HEADS, LANES), F32)
        for _ in range(MOBA_TOPK):
            m = jnp.max(g, axis=-1, keepdims=True)
            idx = jnp.min(jnp.where(g == m, lane, LANES), axis=-1, keepdims=True)
            pick = lane == idx
            chosen = jnp.where(pick & (lane < nblk), 1.0, chosen)
            g = jnp.where(pick, NEG_INF, g)
        sel_ref[0] = chosen


def _dec_attend_kernel(pt_ref, blist_ref, nsel_ref, sel_ref, q_ref, kn_ref, vn_ref, *refs, bps):
    del pt_ref
    s_refs = refs[:bps]
    v_refs = refs[bps:3 * bps]
    o_ref, m_scr, l_scr, acc_scr = refs[3 * bps:]
    b = pl.program_id(0)
    j = pl.program_id(1)

    @pl.when(j == 0)
    def _():
        for h in range(ATT_HEADS):
            m_scr[h:h + 1, :] = jnp.sum(q_ref[0, h] * kn_ref[0, h], axis=0, keepdims=True) * (ATT_HEADDIM ** -0.5)
        l_scr[...] = jnp.ones_like(l_scr)
        acc_scr[...] = vn_ref[0]

    for u in range(bps):
        k = j * bps + u

        @pl.when(k < nsel_ref[b])
        def _(u=u, k=k):
            lane = _iota((ATT_HEADS, LANES), 1)
            on = jnp.sum(jnp.where(lane == blist_ref[b, k], sel_ref[0], 0.0), axis=-1, keepdims=True) > 0.5
            s = jnp.where(on, s_refs[u][0, 0], NEG_INF)
            m = m_scr[...]
            m_new = jnp.maximum(m, jnp.max(s, axis=-1, keepdims=True))
            p = jnp.exp(s - m_new)
            alpha = jnp.exp(m - m_new)
            m_scr[...] = m_new
            l_scr[...] = alpha * l_scr[...] + jnp.sum(p, axis=-1, keepdims=True)
            v0_ref, v1_ref = v_refs[2 * u], v_refs[2 * u + 1]
            for h in range(ATT_HEADS):
                pv = jnp.sum(v0_ref[h] * p[h:h + 1, 0:PAGE_SIZE] + v1_ref[h] * p[h:h + 1, PAGE_SIZE:],
                             axis=-1, keepdims=True)
                acc_scr[h] = alpha[h:h + 1, :] * acc_scr[h] + pv

    @pl.when(j == pl.num_programs(1) - 1)
    def _():
        for h in range(ATT_HEADS):
            o_ref[0, h] = acc_scr[h] / l_scr[h:h + 1, :]


def _dec_attn(q_rot, k_new, v_new, pool_k, pool_v, page_table):
    nb = q_rot.shape[0]
    n_pages = page_table.shape[1]
    per = MOBA_BLOCK // PAGE_SIZE
    assert per == 2
    nblk = n_pages // per
    assert MOBA_TOPK <= nblk <= LANES
    col_shape = (1, ATT_HEADS, ATT_HEADDIM, 1)
    page_shape = (None, ATT_HEADS, ATT_HEADDIM, PAGE_SIZE)
    s_shape = (1, 1, ATT_HEADS, MOBA_BLOCK)

    bps = DEC_BLOCKS_PER_STEP
    assert nblk % bps == 0
    k_page = lambda r: pl.BlockSpec(page_shape, lambda b, j, pt: (pt[b, per * bps * j + r], 0, 0, 0))
    gate_spec = pltpu.PrefetchScalarGridSpec(
        num_scalar_prefetch=1,
        grid=(nb, nblk // bps),
        in_specs=[pl.BlockSpec(col_shape, lambda b, j, pt: (b, 0, 0, 0))] + [k_page(r) for r in range(per * bps)],
        out_specs=[pl.BlockSpec((1, bps, ATT_HEADS, MOBA_BLOCK), lambda b, j, pt: (b, j, 0, 0)),
                   pl.BlockSpec((1, ATT_HEADS, LANES), lambda b, j, pt: (b, 0, 0))],
        scratch_shapes=[pltpu.VMEM((ATT_HEADS, ATT_HEADDIM, PAGE_SIZE), F32), pltpu.VMEM((ATT_HEADS, LANES), F32)])
    scores, sel = pl.pallas_call(
        functools.partial(_dec_gate_kernel, nblk=nblk, bps=bps),
        grid_spec=gate_spec,
        out_shape=[jax.ShapeDtypeStruct((nb, nblk, ATT_HEADS, MOBA_BLOCK), F32),
                   jax.ShapeDtypeStruct((nb, ATT_HEADS, LANES), F32)],
        compiler_params=_params(("parallel", "arbitrary")),
        name="dec_gate",
    )(page_table, q_rot, *([pool_k] * (per * bps)))

    flag = (jnp.max(sel[:, :, :nblk], axis=1) > 0.5).astype(I32)
    nsel = jnp.sum(flag, axis=1)
    abps = DEC_ATTEND_BLOCKS_PER_STEP
    nsteps = -(-min(nblk, ATT_HEADS * MOBA_TOPK) // abps)
    order = jnp.argsort(1 - flag, axis=1, stable=True).astype(I32)
    slot = jnp.minimum(jnp.arange(nsteps * abps, dtype=I32)[None, :], (nsel - 1)[:, None])
    blist = jnp.take_along_axis(order, slot, axis=1)

    col = pl.BlockSpec(col_shape, lambda b, j, pt, bl, ns: (b, 0, 0, 0))
    s_blk = lambda u: pl.BlockSpec(s_shape, lambda b, j, pt, bl, ns: (b, bl[b, j * abps + u], 0, 0))
    v_page = lambda u, half: pl.BlockSpec(
        page_shape, lambda b, j, pt, bl, ns: (pt[b, per * bl[b, j * abps + u] + half], 0, 0, 0))

    attend_spec = pltpu.PrefetchScalarGridSpec(
        num_scalar_prefetch=3,
        grid=(nb, nsteps),
        in_specs=([pl.BlockSpec((1, ATT_HEADS, LANES), lambda b, j, pt, bl, ns: (b, 0, 0)), col, col, col]
                  + [s_blk(u) for u in range(abps)]
                  + [v_page(u, half) for u in range(abps) for half in range(per)]),
        out_specs=col,
        scratch_shapes=[pltpu.VMEM((ATT_HEADS, 1), F32), pltpu.VMEM((ATT_HEADS, 1), F32),
                        pltpu.VMEM((ATT_HEADS, ATT_HEADDIM, 1), F32)])
    return pl.pallas_call(
        functools.partial(_dec_attend_kernel, bps=abps),
        grid_spec=attend_spec,
        out_shape=jax.ShapeDtypeStruct((nb, ATT_HEADS, ATT_HEADDIM, 1), F32),
        compiler_params=_params(("parallel", "arbitrary")),
        name="dec_attend",
    )(page_table, blist, nsel, sel, q_rot, k_new, v_new, *([scores] * abps), *([pool_v] * (abps * per)))


def _headnorm_kernel(x_ref, w_ref, o_ref):
    for h in range(MEM_HEADS):
        sl = slice(h * MEM_HEADDIM, (h + 1) * MEM_HEADDIM)
        x = x_ref[:, sl]
        ms = jnp.mean(x * x, axis=-1, keepdims=True)
        o_ref[:, sl] = x * lax.rsqrt(ms + EPS) * w_ref[...]


def _headnorm(x2, w, tm):
    t = x2.shape[0]
    return pl.pallas_call(
        _headnorm_kernel,
        grid=(t // tm,),
        in_specs=[pl.BlockSpec((tm, D_MODEL), lambda i: (i, 0)), pl.BlockSpec((1, MEM_HEADDIM), lambda i: (0, 0))],
        out_specs=pl.BlockSpec((tm, D_MODEL), lambda i: (i, 0)),
        out_shape=jax.ShapeDtypeStruct((t, D_MODEL), F32),
        compiler_params=_params(("parallel",)),
        name="headnorm",
    )(x2, w)


def _mem_kernel(q_ref, k_ref, v_ref, qw_ref, o_ref):
    kb = k_ref[0].astype(BF16)
    vb = v_ref[0].astype(BF16)
    for h in range(MEM_HEADS):
        sl = slice(h * MEM_HEADDIM, (h + 1) * MEM_HEADDIM)
        q = q_ref[0, :, sl]
        ms = jnp.mean(q * q, axis=-1, keepdims=True)
        qn = (q * lax.rsqrt(ms + EPS) * qw_ref[...]).astype(BF16)
        s = _dot_nt(qn, kb[:, sl]) * (MEM_HEADDIM ** -0.5)
        e = jnp.exp(s - jnp.max(s, axis=-1, keepdims=True))
        p = e / jnp.sum(e, axis=-1, keepdims=True)
        o_ref[0, :, sl] = _dot(p.astype(BF16), vb[:, sl]).astype(BF16)


def _mem_attend(q3, qcol, mk, mv, qw, tq):
    nb, s, _ = q3.shape
    m = mk.shape[1]
    return pl.pallas_call(
        _mem_kernel,
        grid=(nb, s // tq),
        in_specs=[pl.BlockSpec((1, tq, D_MODEL), lambda b, i: (b, i, qcol)),
                  pl.BlockSpec((1, m, D_MODEL), lambda b, i: (b, 0, 0)),
                  pl.BlockSpec((1, m, D_MODEL), lambda b, i: (b, 0, 0)),
                  pl.BlockSpec((1, MEM_HEADDIM), lambda b, i: (0, 0))],
        out_specs=pl.BlockSpec((1, tq, D_MODEL), lambda b, i: (b, i, 0)),
        out_shape=jax.ShapeDtypeStruct((nb, s, D_MODEL), BF16),
        compiler_params=_params(("parallel", "parallel")),
        name="mem_attend",
    )(q3, mk, mv, qw)


def _merge_kernel(ys_ref, oa_ref, om_ref, g1_ref, g2_ref, g3_ref, x_ref, ws_ref, wa_ref, wm_ref, wo_ref, nw_ref,
                  x1_ref, hf_ref):
    merged = (_sigmoid(g1_ref[...]) * _dot(ys_ref[...], ws_ref[...])
              + _sigmoid(g2_ref[...]) * _dot(oa_ref[...], wa_ref[...])
              + _sigmoid(g3_ref[...]) * _dot(om_ref[...], wm_ref[...]))
    x1 = x_ref[...] + _dot(merged.astype(BF16), wo_ref[...])
    x1_ref[...] = x1
    ms = jnp.mean(x1 * x1, axis=-1, keepdims=True)
    hf_ref[...] = (x1 * lax.rsqrt(ms + EPS) * nw_ref[...]).astype(BF16)


def _merge(y_ssd, o_att, o_mem, p2, x2, ws, wa, wm, wo, nw, tm):
    t = x2.shape[0]
    tok = lambda w, c=0: pl.BlockSpec((tm, w), lambda i: (i, c))
    res = lambda a: pl.BlockSpec(a.shape, lambda i: (0, 0))
    return pl.pallas_call(
        _merge_kernel,
        grid=(t // tm,),
        in_specs=[tok(SSD_INNER), tok(D_MODEL), tok(D_MODEL),
                  tok(D_MODEL, COL_G), tok(D_MODEL, COL_G + 1), tok(D_MODEL, COL_G + 2), tok(D_MODEL),
                  res(ws), res(wa), res(wm), res(wo), res(nw)],
        out_specs=[tok(D_MODEL), tok(D_MODEL)],
        out_shape=[jax.ShapeDtypeStruct((t, D_MODEL), F32), jax.ShapeDtypeStruct((t, D_MODEL), BF16)],
        compiler_params=_params(("parallel",)),
        name="merge",
    )(y_ssd, o_att, o_mem, p2, p2, p2, x2, ws, wa, wm, wo, nw)


def _top16(srcs, n_rows, tp):
    r_id = _iota((n_rows, tp), 0)
    slot = _iota((PEER_TOPK, tp), 0)

    def body(r, carry):
        out = []
        for src_scr, (vals, idxs) in zip(srcs, carry):
            sc = src_scr[...]
            m = jnp.max(sc, axis=0, keepdims=True)
            idx = jnp.min(jnp.where(sc == m, r_id, n_rows), axis=0, keepdims=True)
            src_scr[...] = jnp.where(r_id == idx, NEG_INF, sc)
            out.append((jnp.where(slot == r, m, vals), jnp.where(slot == r, idx, idxs)))
        return tuple(out)

    zero = (jnp.zeros((PEER_TOPK, tp), F32), jnp.zeros((PEER_TOPK, tp), I32))
    return lax.fori_loop(0, PEER_TOPK, body, tuple(zero for _ in srcs))


def _peer_sel_kernel(hf_ref, wqt_ref, keys_ref, a_ref, b_ref, g_ref,
                     qt_scr, sc_scr, sv_scr, si_scr, cand_scr, oa_scr, ob_scr, og_scr):
    tp = hf_ref.shape[0]
    nk = PEER_NKEYS
    kk = PEER_TOPK
    qt_scr[...] = _dot_nt(wqt_ref[...], hf_ref[...])

    def halves(h, carry):
        for c in range(2):
            hc = 2 * h + c
            q_hc = qt_scr[pl.ds(pl.multiple_of(hc * nk, nk), nk), :]
            sc_scr[c] = _dot3(keys_ref[hc], q_hc, _dot)
        tops = _top16([sc_scr.at[0], sc_scr.at[1]], nk, tp)
        for c in range(2):
            sv_scr[2 * h + c] = tops[c][0]
            si_scr[2 * h + c] = tops[c][1]
        return carry

    lax.fori_loop(0, PEER_HEADS, halves, 0)

    def head(h, carry):
        sv1 = sv_scr[2 * h]
        sv2 = sv_scr[2 * h + 1]
        si1 = si_scr[2 * h]
        si2 = si_scr[2 * h + 1]
        cand_scr[0:kk, :] = sv1[0:1, :] + sv2
        b8 = _iota((SUBLANES, tp), 0)
        for a in range(1, SUBLANES):
            c = sv1[a:a + 1, :] + sv2[0:SUBLANES, :]
            n_b = kk // (a + 1)
            cand_scr[SUBLANES * (a + 1):SUBLANES * (a + 2), :] = c if n_b >= SUBLANES else jnp.where(b8 < n_b, c, NEG_INF)
        cand_scr[PEER_NCAND - SUBLANES:, :] = sv1[SUBLANES:, :] + sv2[0:1, :]
        (fv, fr), = _top16([cand_scr], PEER_NCAND, tp)
        mid = fr - kk
        fa = jnp.where(fr < kk, 0, jnp.where(fr < PEER_NCAND - SUBLANES, 1 + (mid >> 3), fr - (PEER_NCAND - 2 * SUBLANES)))
        fb = jnp.where(fr < kk, fr, jnp.where(fr < PEER_NCAND - SUBLANES, mid & (SUBLANES - 1), 0))
        e1 = jnp.zeros((kk, tp), I32)
        e2 = jnp.zeros((kk, tp), I32)
        for a in range(kk):
            e1 = e1 + jnp.where(fa == a, si1[a:a + 1, :], 0)
            e2 = e2 + jnp.where(fb == a, si2[a:a + 1, :], 0)
        e = jnp.exp(fv - fv[0:1, :])
        rows = pl.ds(pl.multiple_of(h * kk, kk), kk)
        oa_scr[rows, :] = e1.astype(F32)
        ob_scr[rows, :] = e2.astype(F32)
        og_scr[rows, :] = e / jnp.sum(e, axis=0, keepdims=True)
        return carry

    lax.fori_loop(0, PEER_HEADS, head, 0)
    a_ref[...] = oa_scr[...].T
    b_ref[...] = ob_scr[...].T
    g_ref[...] = og_scr[...].T


def _peer_sel(hf, wqt, keys, tp):
    t = hf.shape[0]
    kk = PEER_TOPK
    npick = PEER_HEADS * kk
    out = pl.BlockSpec((tp, npick), lambda i: (i, 0))
    return pl.pallas_call(
        _peer_sel_kernel,
        grid=(t // tp,),
        in_specs=[pl.BlockSpec((tp, D_MODEL), lambda i: (i, 0)),
                  pl.BlockSpec(wqt.shape, lambda i: (0, 0)),
                  pl.BlockSpec(keys.shape, lambda i: (0, 0, 0))],
        out_specs=[out, out, out],
        out_shape=[jax.ShapeDtypeStruct((t, npick), F32)] * 3,
        scratch_shapes=[pltpu.VMEM((wqt.shape[0], tp), F32),
                        pltpu.VMEM((2, PEER_NKEYS, tp), F32),
                        pltpu.VMEM((2 * PEER_HEADS, kk, tp), F32),
                        pltpu.VMEM((2 * PEER_HEADS, kk, tp), I32),
                        pltpu.VMEM((PEER_NCAND, tp), F32),
                        pltpu.VMEM((npick, tp), F32),
                        pltpu.VMEM((npick, tp), F32),
                        pltpu.VMEM((npick, tp), F32)],
        compiler_params=_params(("parallel",)),
        name="peer_sel",
    )(hf, wqt, keys)


def _peer_kernel(hf_ref, a_ref, b_ref, g_ref, x1_ref, u_ref, v_ref, o_ref, g_scr, *, tm, te):
    nk = PEER_NKEYS
    e = pl.program_id(1)
    n1 = te // nk

    @pl.when(e == 0)
    def _():
        o_ref[...] = x1_ref[...]
        sub = _iota((nk, LANES), 0).astype(F32)

        def toks(tg, carry):
            for u in range(PEER_BUILD_TOKENS):
                t = tg * PEER_BUILD_TOKENS + u
                row = pl.ds(t, 1)
                y = jnp.where(sub == a_ref[row, :], g_ref[row, :], 0.0).astype(BF16)
                z = jnp.where(sub == b_ref[row, :], 1.0, 0.0).astype(BF16)
                g_scr[pl.ds(pl.multiple_of(t * nk, nk), nk), :] = _dot_nt(y, z)
            return carry

        lax.fori_loop(0, tm // PEER_BUILD_TOKENS, toks, 0)

    hf = hf_ref[...]
    ck = PEER_EXPERT_CHUNK
    for c in range(te // ck):
        act = _dot_nt(hf, u_ref[c * ck:(c + 1) * ck, :])
        act = 0.5 * act * (1.0 + lax.erf(act * (2.0 ** -0.5)))
        ws = []
        for k in range(ck // nk):
            gk = g_scr[pl.ds(e * n1 + c * (ck // nk) + k, tm, stride=nk), :]
            ws.append((act[:, k * nk:(k + 1) * nk] * gk).astype(BF16))
        o_ref[...] += _dot(jnp.concatenate(ws, axis=1), v_ref[c * ck:(c + 1) * ck, :])


def _peer(hf, a, b, g, x1, u, v, tm, te):
    t = hf.shape[0]
    n_exp = u.shape[0]
    npick = a.shape[1]
    tok = lambda w: pl.BlockSpec((tm, w), lambda i, e: (i, 0))
    exp = pl.BlockSpec((te, D_MODEL), lambda i, e: (e, 0))
    return pl.pallas_call(
        functools.partial(_peer_kernel, tm=tm, te=te),
        grid=(t // tm, n_exp // te),
        in_specs=[tok(D_MODEL), tok(npick), tok(npick), tok(npick), tok(D_MODEL), exp, exp],
        out_specs=tok(D_MODEL),
        out_shape=jax.ShapeDtypeStruct((t, D_MODEL), F32),
        scratch_shapes=[pltpu.VMEM((tm * PEER_NKEYS, PEER_NKEYS), F32)],
        compiler_params=_params(("parallel", "arbitrary")),
        name="peer",
    )(hf, a, b, g, x1, u, v)


def _pack_w_in(w_in):
    n0 = SSD_INNER + SSD_CONV_DIM
    w_main = jnp.concatenate([w_in[:, :n0], w_in[:, n0 + SSD_HEADS:]], axis=1).astype(BF16)
    w_dt = jnp.pad(w_in[:, n0:n0 + SSD_HEADS], ((0, 0), (0, LANES - SSD_HEADS))).astype(BF16)
    return w_main, w_dt


def _pad_lanes(v):
    return jnp.pad(v, (0, LANES - v.shape[0]))[None]


def kernel(x_prompt, x_sample, mem_prompt, cache_k, cache_v, page_table, cache_mem_k, cache_mem_v, state_ssm, state_conv, norm1_w, w_in, conv_w, conv_b, dt_bias, a_log, d_skip, ssd_norm_w, q_norm_w, k_norm_w, mem_norm_w, w_mem_k, w_mem_v, mem_qn_w, mem_kn_w, w_br_ssd, w_br_attn, w_br_mem, w_out, norm2_w, peer_wq, peer_keys, peer_u, peer_v):
    bp, sp, _ = x_prompt.shape
    bs, ss, _ = x_sample.shape
    assert ss == 1 and cache_k.shape[0] == 1
    tp, ts = bp * sp, bs * ss

    w_main, w_dt = _pack_w_in(w_in[0])
    nw1 = norm1_w[0][None]
    ssd_w = (conv_w[0], conv_b[0][None], _pad_lanes(dt_bias[0]), _pad_lanes(a_log[0]),
             jnp.repeat(d_skip[0], SSD_HEADDIM)[None], ssd_norm_w[0][None])
    qw2 = jnp.tile(q_norm_w[0], LANES // ATT_HEADDIM)[None]
    kw2 = jnp.tile(k_norm_w[0], LANES // ATT_HEADDIM)[None]
    w_mem_kv = jnp.concatenate([w_mem_k[0], w_mem_v[0]], axis=1).astype(BF16)
    merge_w = (w_br_ssd[0].astype(BF16), w_br_attn[0].astype(BF16), w_br_mem[0].astype(BF16),
               w_out[0].astype(BF16), norm2_w[0][None])
    wqt = peer_wq[0].T.astype(BF16)
    keys = peer_keys[0].reshape(2 * PEER_HEADS, PEER_NKEYS, -1)
    u_bf = peer_u[0].astype(BF16)
    v_bf = peer_v[0].astype(BF16)
    mem_qw = mem_qn_w[0][None]

    xp = x_prompt.reshape(tp, D_MODEL)
    p_main = _proj(xp, nw1, w_main, 1024, 1024)
    p_dt = _proj(xp, nw1, w_dt, 1024, LANES)
    p3 = p_main.reshape(bp, sp, N_MAIN)
    y_ssd, ssm_p = _ssd(p3, p_dt.reshape(bp, sp, LANES), jnp.zeros((bp, SUBLANES, SSD_CONV_DIM), F32),
                        jnp.zeros((bp, SSD_HEADS, SSD_HEADDIM, SSD_STATE), F32), *ssd_w, SSD_CHUNK)
    cos, sin = _rope_tables(jnp.arange(sp, dtype=I32))
    qa, k_rot, ka, v_p, vt, sel = _qk(p3, cos, sin, qw2, kw2)
    o_att = _attn(qa, ka, vt, sel)
    n_mem = mem_prompt.shape[1]
    kv = _proj(mem_prompt.reshape(bp * n_mem, D_MODEL), mem_norm_w[0][None], w_mem_kv, 256, D_MODEL)
    mem_k = _headnorm(kv, mem_kn_w[0][None], 256)
    mem_v = kv[:, D_MODEL:]
    o_mem = _mem_attend(p3, COL_MQ, mem_k.reshape(bp, n_mem, D_MODEL), mem_v.reshape(bp, n_mem, D_MODEL), mem_qw, 512)
    x1, hf = _merge(y_ssd.reshape(tp, SSD_INNER), o_att.reshape(tp, D_MODEL), o_mem.reshape(tp, D_MODEL),
                    p_main, xp, *merge_w, 512)
    pa, pb, pg = _peer_sel(hf, wqt, keys, 256)
    y_p = _peer(hf, pa, pb, pg, x1, u_bf, v_bf, 256, 2048)

    xs = x_sample.reshape(ts, D_MODEL)
    ps_main = _proj(xs, nw1, w_main, ts, 1024)
    ps_dt = _proj(xs, nw1, w_dt, ts, LANES)
    ps3 = ps_main.reshape(bs, 1, N_MAIN)
    conv_prev8 = jnp.pad(state_conv[0], ((0, 0), (SUBLANES - (SSD_CONV - 1), 0), (0, 0)))
    ys_ssd, ssm_s = _ssd(ps3, ps_dt.reshape(bs, 1, LANES), conv_prev8, state_ssm[0], *ssd_w, 1)
    cos_s, sin_s = _rope_tables(jnp.full((ts,), PAST_LEN, I32))
    q_rot_s, k_rot_s = _qk_s(ps_main, cos_s, sin_s, qw2, kw2)
    v_s = ps_main[:, COL_V * D_MODEL:(COL_V + 1) * D_MODEL]
    cols = lambda v: v.reshape(bs, ATT_HEADS, ATT_HEADDIM, 1)
    pool = lambda c: jnp.transpose(c[0], (0, 2, 3, 1))
    o_att_s = _dec_attn(cols(q_rot_s), cols(k_rot_s), cols(v_s), pool(cache_k), pool(cache_v), page_table)
    mq8 = jnp.broadcast_to(ps3[:, :, COL_MQ * D_MODEL:(COL_MQ + 1) * D_MODEL], (bs, SUBLANES, D_MODEL))
    o_mem_s = _mem_attend(mq8, 0, cache_mem_k[0].reshape(bs, n_mem, D_MODEL),
                          cache_mem_v[0].reshape(bs, n_mem, D_MODEL), mem_qw, SUBLANES)[:, 0]
    x1_s, hf_s = _merge(ys_ssd.reshape(ts, SSD_INNER), o_att_s.reshape(ts, D_MODEL).astype(BF16), o_mem_s,
                        ps_main, xs, *merge_w, ts)
    pad_rows = lambda v: jnp.pad(v, ((0, LANES - ts), (0, 0)))
    hf_sp = pad_rows(hf_s)
    sa, sb, sg = _peer_sel(hf_sp, wqt, keys, LANES)
    y_s = _peer(hf_sp, sa, sb, sg, pad_rows(x1_s), u_bf, v_bf, LANES, 2048)[:ts]

    kv_shape = (1, bp, sp, ATT_HEADS, ATT_HEADDIM)
    mem_shape = (1, bp, n_mem, MEM_HEADS, MEM_HEADDIM)
    xbc_cols = slice(COL_X * 1024, COL_X * 1024 + SSD_CONV_DIM)
    conv_p = p3[:, sp - (SSD_CONV - 1):, xbc_cols]
    conv_s = jnp.concatenate([state_conv[0][:, 1:], ps3[:, :, xbc_cols]], axis=1)
    return (y_p.reshape(bp, sp, D_MODEL), y_s.reshape(bs, ss, D_MODEL),
            k_rot.reshape(kv_shape), v_p.reshape(kv_shape),
            mem_k.reshape(mem_shape), mem_v.reshape(mem_shape),
            ssm_p[None], conv_p[None],
            k_rot_s.reshape(1, bs, ss, ATT_HEADS, ATT_HEADDIM), v_s.reshape(1, bs, ss, ATT_HEADS, ATT_HEADDIM),
            ssm_s[None], conv_s[None])
```

```python
import functools
import math

import jax
import jax.numpy as jnp
from jax import lax
from jax.experimental import pallas as pl
from jax.experimental.pallas import tpu as pltpu

F32 = jnp.float32
BF16 = jnp.bfloat16
I32 = jnp.int32
EPS = 1e-6
NEG_INF = float("-inf")

D_MODEL = 1024
SSD_INNER = 2048
SSD_HEADS = 32
SSD_HEADDIM = 64
SSD_GROUPS = 8
SSD_STATE = 128
SSD_CONV = 4
SSD_CHUNK = 128
SSD_CONV_DIM = 4096
ATT_HEADS = 16
ATT_HEADDIM = 64
MOBA_BLOCK = 256
MOBA_TOPK = 3
ROPE_THETA = 10000.0
MEM_HEADS = 4
MEM_HEADDIM = 256
PEER_HEADS = 8
PEER_NKEYS = 128
PEER_TOPK = 16
PEER_EXPERT_CHUNK = 512
PEER_BUILD_TOKENS = 16
PEER_NCAND = 80
PAST_LEN = 16384
PAGE_SIZE = 128
DEC_BLOCKS_PER_STEP = 4
DEC_ATTEND_BLOCKS_PER_STEP = 2

LANES = 128
SUBLANES = 8
VMEM_LIMIT = 56 * 1024 * 1024

COL_Z, COL_X, COL_BC, COL_Q, COL_K, COL_V, COL_MQ, COL_G = 0, 2, 4, 6, 7, 8, 9, 10
N_MAIN = 13 * 1024


def _params(sem):
    return pltpu.CompilerParams(dimension_semantics=sem, vmem_limit_bytes=VMEM_LIMIT)


def _dot(a, b):
    return jnp.dot(a, b, preferred_element_type=F32)


def _dot_nt(a, b):
    return lax.dot_general(a, b, (((1,), (1,)), ((), ())), preferred_element_type=F32)


def _dot_tn(a, b):
    return lax.dot_general(a, b, (((0,), (0,)), ((), ())), preferred_element_type=F32)


def _split(a):
    hi = a.astype(BF16)
    lo = (a - hi.astype(F32)).astype(BF16)
    return hi, lo


def _dot3(a, b, dot):
    ah, al = _split(a)
    bh, bl = _split(b)
    return dot(ah, bh) + (dot(ah, bl) + dot(al, bh))


def _sigmoid(x):
    return 1.0 / (1.0 + jnp.exp(-x))


def _iota(shape, axis):
    return lax.broadcasted_iota(I32, shape, axis)


def _tree(x, op):
    while x.shape[0] > 1:
        h = x.shape[0] // 2
        x = op(x[:h], x[h:])
    return x[0]


def _proj_kernel(x_ref, nw_ref, w_ref, o_ref, h_scr):
    @pl.when(pl.program_id(1) == 0)
    def _():
        x = x_ref[...]
        ms = jnp.mean(x * x, axis=-1, keepdims=True)
        h_scr[...] = (x * lax.rsqrt(ms + EPS) * nw_ref[...]).astype(BF16)

    o_ref[...] = _dot(h_scr[...], w_ref[...])


def _proj(x, nw, w, tm, tn):
    t, d = x.shape
    n = w.shape[1]
    return pl.pallas_call(
        _proj_kernel,
        grid=(t // tm, n // tn),
        in_specs=[pl.BlockSpec((tm, d), lambda i, j: (i, 0)),
                  pl.BlockSpec((1, d), lambda i, j: (0, 0)),
                  pl.BlockSpec((d, tn), lambda i, j: (0, j))],
        out_specs=pl.BlockSpec((tm, tn), lambda i, j: (i, j)),
        out_shape=jax.ShapeDtypeStruct((t, n), F32),
        scratch_shapes=[pltpu.VMEM((tm, d), BF16)],
        compiler_params=_params(("parallel", "arbitrary")),
        name="proj",
    )(x, nw, w)


def _ssd_kernel(z_ref, x_ref, bc_ref, dt_ref, cprev_ref, h0_ref, cw_ref, cb_ref, dtb_ref, alog_ref,
                dsk_ref, nw_ref, y_ref, hl_ref, buf, st, ycat, *, rows):
    L = SSD_CHUNK
    c = pl.program_id(1)

    @pl.when(c == 0)
    def _():
        buf[0:SUBLANES, :] = cprev_ref[0]
        st[...] = h0_ref[0].reshape(st.shape)

    if rows < L:
        buf[SUBLANES:, :] = jnp.zeros((L, SSD_CONV_DIM), F32)
    buf[SUBLANES:SUBLANES + rows, 0:SSD_INNER] = x_ref[0]
    buf[SUBLANES:SUBLANES + rows, SSD_INNER:] = bc_ref[0]

    acc = buf[SUBLANES:SUBLANES + L, :] * cw_ref[SSD_CONV - 1:SSD_CONV, :]
    for j in range(SSD_CONV - 1):
        off = SUBLANES - (SSD_CONV - 1) + j
        acc = acc + buf[off:off + L, :] * cw_ref[j:j + 1, :]
    acc = acc + cb_ref[...]
    xbc = acc * _sigmoid(acc)
    buf[0:SUBLANES, :] = buf[L:L + SUBLANES, :]

    xs = xbc[:, :SSD_INNER]
    n_gn = SSD_GROUPS * SSD_STATE
    bm = xbc[:, SSD_INNER:SSD_INNER + n_gn].astype(BF16)
    cm = xbc[:, SSD_INNER + n_gn:].astype(BF16)

    dt_raw = dt_ref[0] if rows == L else jnp.broadcast_to(dt_ref[0], (L, LANES))
    pre = dt_raw + dtb_ref[...]
    dt = jnp.maximum(pre, 0.0) + jnp.log1p(jnp.exp(-jnp.abs(pre)))
    row_id = _iota((L, LANES), 0)
    dt = jnp.where(row_id < rows, dt, 0.0)
    a = -jnp.exp(alog_ref[...])
    da = dt * a
    tri = _iota((L, L), 0) >= _iota((L, L), 1)
    a_cs = jnp.dot(tri.astype(F32), da, preferred_element_type=F32, precision=lax.Precision.HIGHEST)
    a_cs_t = a_cs.T
    a_last = a_cs[L - 1:L, :]
    e_cs = jnp.exp(a_cs)
    d_end = jnp.exp(a_last - a_cs)
    e_last = jnp.exp(a_last)

    head_lanes = (_iota((LANES, SSD_INNER), 1) // SSD_HEADDIM == _iota((LANES, SSD_INNER), 0))
    head_lanes = jnp.where(head_lanes, 1.0, 0.0).astype(BF16)

    def spread(v):
        p1 = v.astype(BF16)
        r1 = v - p1.astype(F32)
        p2 = r1.astype(BF16)
        p3 = (r1 - p2.astype(F32)).astype(BF16)
        return _dot(p1, head_lanes) + (_dot(p2, head_lanes) + _dot(p3, head_lanes))

    xdt_all = xs * spread(dt)
    xdt_bf = xdt_all.astype(BF16)
    xw_bf = (xdt_all * spread(d_end)).astype(BF16)
    e_cs_all = spread(e_cs)
    left = _iota((L, LANES), 1) < SSD_HEADDIM
    top = _iota((LANES, SSD_STATE), 0) < SSD_HEADDIM

    for g in range(SSD_GROUPS):
        bg = bm[:, g * SSD_STATE:(g + 1) * SSD_STATE]
        cg = cm[:, g * SSD_STATE:(g + 1) * SSD_STATE]
        cb = _dot_nt(cg, bg)
        for r in range(SSD_HEADS // SSD_GROUPS // 2):
            pair = g * (SSD_HEADS // SSD_GROUPS // 2) + r
            ps = slice(pair * LANES, (pair + 1) * LANES)
            ms_ = []
            for h in (2 * pair, 2 * pair + 1):
                decay = jnp.where(tri, jnp.exp(a_cs[:, h:h + 1] - a_cs_t[h:h + 1, :]), 0.0)
                ms_.append((cb * decay).astype(BF16))
            xp = xdt_bf[:, ps]
            yd = jnp.where(left, _dot(ms_[0], xp), _dot(ms_[1], xp))
            hp = st[pair]
            yo = _dot_nt(cg, hp.astype(BF16)) * e_cs_all[:, ps]
            keep = jnp.where(top, e_last[:, 2 * pair:2 * pair + 1], e_last[:, 2 * pair + 1:2 * pair + 2])
            st[pair] = hp * keep + _dot_tn(xw_bf[:, ps], bg)
            ycat[:, ps] = yd + yo + xs[:, ps] * dsk_ref[:, ps]

    z = z_ref[0]
    if rows < L:
        y = ycat[0:rows, :] * (z * _sigmoid(z))
    else:
        y = ycat[...] * (z * _sigmoid(z))
    ms = jnp.mean(y * y, axis=-1, keepdims=True)
    y_ref[0] = (y * lax.rsqrt(ms + EPS) * nw_ref[...]).astype(BF16)

    @pl.when(c == pl.num_programs(1) - 1)
    def _():
        hl_ref[0] = st[...].reshape(SSD_HEADS, SSD_HEADDIM, SSD_STATE)


def _ssd(p3, dt3, conv_prev8, h0, cw, cb, dtb, alog, dsk, nw, rows):
    nb, s, _ = p3.shape
    nc = s // rows
    w2 = SSD_INNER
    const = lambda shape: pl.BlockSpec(shape, lambda b, c: tuple(0 for _ in shape))
    return pl.pallas_call(
        functools.partial(_ssd_kernel, rows=rows),
        grid=(nb, nc),
        in_specs=[pl.BlockSpec((1, rows, w2), lambda b, c: (b, c, COL_Z // 2)),
                  pl.BlockSpec((1, rows, w2), lambda b, c: (b, c, COL_X // 2)),
                  pl.BlockSpec((1, rows, w2), lambda b, c: (b, c, COL_BC // 2)),
                  pl.BlockSpec((1, rows, LANES), lambda b, c: (b, c, 0)),
                  pl.BlockSpec((1, SUBLANES, SSD_CONV_DIM), lambda b, c: (b, 0, 0)),
                  pl.BlockSpec((1, SSD_HEADS, SSD_HEADDIM, SSD_STATE), lambda b, c: (b, 0, 0, 0)),
                  const((SSD_CONV, SSD_CONV_DIM)), const((1, SSD_CONV_DIM)), const((1, LANES)),
                  const((1, LANES)), const((1, SSD_INNER)), const((1, SSD_INNER))],
        out_specs=[pl.BlockSpec((1, rows, w2), lambda b, c: (b, c, 0)),
                   pl.BlockSpec((1, SSD_HEADS, SSD_HEADDIM, SSD_STATE), lambda b, c: (b, 0, 0, 0))],
        out_shape=[jax.ShapeDtypeStruct((nb, s, w2), BF16),
                   jax.ShapeDtypeStruct((nb, SSD_HEADS, SSD_HEADDIM, SSD_STATE), F32)],
        scratch_shapes=[pltpu.VMEM((SUBLANES + SSD_CHUNK, SSD_CONV_DIM), F32),
                        pltpu.VMEM((SSD_HEADS // 2, 2 * SSD_HEADDIM, SSD_STATE), F32),
                        pltpu.VMEM((SSD_CHUNK, SSD_INNER), F32)],
        compiler_params=_params(("parallel", "arbitrary")),
        name="ssd",
    )(p3, p3, p3, dt3, conv_prev8, h0, cw, cb, dtb, alog, dsk, nw)


def _rope_tables(pos):
    d = ATT_HEADDIM
    inv = 1.0 / (ROPE_THETA ** (jnp.arange(0, d, 2, dtype=F32) / d))
    ang = pos.astype(F32)[:, None] * inv[None, :]
    cos = jnp.cos(ang)
    sin = jnp.sin(ang)
    cos = jnp.concatenate([cos, cos, cos, cos], axis=-1)
    sin = jnp.concatenate([-sin, sin, -sin, sin], axis=-1)
    return cos, sin


def _head_norm_rope(x, w, cos, sin, bd, first):
    hi, lo = _split(x * x)
    ms = _dot(hi, bd) + _dot(lo, bd)
    xn = x * lax.rsqrt(ms + EPS) * w
    partner = jnp.where(first, pltpu.roll(xn, LANES - ATT_HEADDIM // 2, 1), pltpu.roll(xn, ATT_HEADDIM // 2, 1))
    return xn * cos + partner * sin


def _head_consts(rows):
    r = _iota((LANES, LANES), 0) // ATT_HEADDIM
    c = _iota((LANES, LANES), 1) // ATT_HEADDIM
    bd = jnp.where(r == c, 1.0 / ATT_HEADDIM, 0.0).astype(BF16)
    first = (_iota((rows, LANES), 1) % ATT_HEADDIM) < (ATT_HEADDIM // 2)
    return bd, first


def _qk_kernel(q_ref, k_ref, v_ref, cos_ref, sin_ref, qw_ref, kw_ref,
               qa_ref, kr_ref, ka_ref, vo_ref, vt_ref, sel_ref, km_scr, qr_scr, *, nblk):
    tb = MOBA_BLOCK
    i = pl.program_id(1)

    @pl.when(i == 0)
    def _():
        km_scr[...] = jnp.zeros_like(km_scr)

    cos = cos_ref[...]
    sin = sin_ref[...]
    bd, first = _head_consts(tb)
    for cb in range(D_MODEL // LANES):
        sl = slice(cb * LANES, (cb + 1) * LANES)
        qr = _head_norm_rope(q_ref[0, :, sl], qw_ref[...], cos, sin, bd, first)
        kr = _head_norm_rope(k_ref[0, :, sl], kw_ref[...], cos, sin, bd, first)
        qr_scr[:, sl] = qr
        qa_ref[0, :, sl] = (qr * (ATT_HEADDIM ** -0.5)).astype(BF16)
        kr_ref[0, :, sl] = kr
        ka_ref[0, :, sl] = kr.astype(BF16)
        k_mean = jnp.sum(kr, axis=0, keepdims=True) * (1.0 / tb)
        km_scr[:, sl] = jnp.where(_iota((nblk, LANES), 0) == i, k_mean, km_scr[:, sl])
    v = v_ref[0]
    vo_ref[0] = v
    vt_ref[0, 0] = v.T.astype(BF16)

    km = km_scr[...]
    lane_head = _iota((nblk, D_MODEL), 1) // ATT_HEADDIM
    kmt = jnp.concatenate([jnp.where(lane_head == h, km, 0.0) for h in range(ATT_HEADS)], axis=0)
    gate = _dot3(kmt, qr_scr[...], _dot_nt)
    n_id = _iota((nblk, tb), 0)
    valid = n_id < i
    for h in range(ATT_HEADS):
        g = jnp.where(valid, gate[h * nblk:(h + 1) * nblk, :], NEG_INF)
        chosen = jnp.zeros((nblk, tb), F32)
        for _ in range(MOBA_TOPK):
            m = jnp.max(g, axis=0, keepdims=True)
            idx = jnp.min(jnp.where(g == m, n_id, nblk), axis=0, keepdims=True)
            pick = n_id == idx
            chosen = jnp.where(pick & valid, 1.0, chosen)
            g = jnp.where(pick, NEG_INF, g)
        sel_ref[0, h] = chosen


def _qk(p3, cos, sin, qw, kw):
    nb, s, _ = p3.shape
    tb = MOBA_BLOCK
    nblk = s // tb
    col = lambda cidx: pl.BlockSpec((1, tb, D_MODEL), lambda b, i: (b, i, cidx))
    out_tok = pl.BlockSpec((1, tb, D_MODEL), lambda b, i: (b, i, 0))
    return pl.pallas_call(
        functools.partial(_qk_kernel, nblk=nblk),
        grid=(nb, nblk),
        in_specs=[col(COL_Q), col(COL_K), col(COL_V),
                  pl.BlockSpec((tb, LANES), lambda b, i: (i, 0)),
                  pl.BlockSpec((tb, LANES), lambda b, i: (i, 0)),
                  pl.BlockSpec((1, LANES), lambda b, i: (0, 0)),
                  pl.BlockSpec((1, LANES), lambda b, i: (0, 0))],
        out_specs=[out_tok, out_tok, out_tok, out_tok,
                   pl.BlockSpec((1, 1, D_MODEL, tb), lambda b, i: (b, i, 0, 0)),
                   pl.BlockSpec((1, ATT_HEADS, nblk, tb), lambda b, i: (b, 0, 0, i))],
        out_shape=[jax.ShapeDtypeStruct((nb, s, D_MODEL), BF16),
                   jax.ShapeDtypeStruct((nb, s, D_MODEL), F32),
                   jax.ShapeDtypeStruct((nb, s, D_MODEL), BF16),
                   jax.ShapeDtypeStruct((nb, s, D_MODEL), F32),
                   jax.ShapeDtypeStruct((nb, nblk, D_MODEL, tb), BF16),
                   jax.ShapeDtypeStruct((nb, ATT_HEADS, nblk, s), F32)],
        scratch_shapes=[pltpu.VMEM((nblk, D_MODEL), F32), pltpu.VMEM((tb, D_MODEL), F32)],
        compiler_params=_params(("parallel", "arbitrary")),
        name="qk",
    )(p3, p3, p3, cos, sin, qw, kw)


def _attn_kernel(q_ref, k_ref, vt_ref, sel_ref, o_ref):
    tb = MOBA_BLOCK
    hd = ATT_HEADDIM
    i = pl.program_id(2)
    nh = LANES // hd
    q = q_ref[0]
    lane_head = _iota((tb, LANES), 1) // hd
    causal = _iota((tb, tb), 0) <= _iota((tb, tb), 1)
    qhs = [jnp.where(lane_head == hh, q, jnp.zeros_like(q)) for hh in range(nh)]

    ones_rows = jnp.ones((2 * SUBLANES, tb), BF16)

    def v_ones(j, hh):
        return jnp.concatenate([vt_ref[0, j, hh * hd:(hh + 1) * hd, :], ones_rows], axis=0)

    n_trips = i // 2 + 1

    def blocks(jj):
        return jnp.where(jj == 0, i, 2 * jj - 1), 2 * jj

    def scores(jj):
        kj = jnp.concatenate([k_ref[0, pl.ds(pl.multiple_of(b * tb, tb), tb), :] for b in blocks(jj)], axis=0)
        return [_dot_nt(kj, qhs[hh]) for hh in range(nh)]

    def trip(jj, carry, first):
        ss = carry[2 * nh:]
        nxt = scores(jnp.minimum(jj + 1, n_trips - 1))
        new = []
        for hh in range(nh):
            m, acc = carry[2 * hh:2 * hh + 2]
            for u, blk in enumerate(blocks(jj)):
                s = ss[hh][u * tb:(u + 1) * tb]
                if first and u == 0:
                    s = jnp.where(causal, s, NEG_INF)
                    smax = jnp.max(s, axis=0, keepdims=True)
                else:
                    on = sel_ref[0, hh, pl.ds(blk, 1), :] > 0.5
                    smax = jnp.where(on, jnp.max(s, axis=0, keepdims=True), NEG_INF)
                m_new = jnp.maximum(m, smax)
                p = jnp.exp(s - m_new)
                pv = _dot(v_ones(blk, hh), p.astype(BF16))
                if first and u == 0:
                    acc = pv
                else:
                    acc = acc * jnp.exp(m - m_new) + jnp.where(on, pv, 0.0)
                m = m_new
            new += [m, acc]
        return tuple(new + nxt)

    state = []
    for hh in range(nh):
        state += [jnp.full((1, tb), NEG_INF, F32), jnp.zeros((hd + 2 * SUBLANES, tb), F32)]
    state = trip(0, tuple(state + scores(0)), True)
    state = lax.fori_loop(1, n_trips, lambda jj, c: trip(jj, c, False), state)
    outs = [state[2 * hh + 1][:hd] / state[2 * hh + 1][hd:hd + 1] for hh in range(nh)]
    o_ref[0] = jnp.concatenate(outs, axis=0).T.astype(BF16)


def _attn(qa, ka, vt, sel):
    nb, s, _ = qa.shape
    tb = MOBA_BLOCK
    nblk = s // tb
    npair = D_MODEL // LANES
    hp = LANES // ATT_HEADDIM
    return pl.pallas_call(
        _attn_kernel,
        grid=(nb, npair, nblk),
        in_specs=[pl.BlockSpec((1, tb, LANES), lambda b, p, i: (b, i, p)),
                  pl.BlockSpec((1, s, LANES), lambda b, p, i: (b, 0, p)),
                  pl.BlockSpec((1, nblk, LANES, tb), lambda b, p, i: (b, 0, p, 0)),
                  pl.BlockSpec((1, hp, nblk, tb), lambda b, p, i: (b, p, 0, i))],
        out_specs=pl.BlockSpec((1, tb, LANES), lambda b, p, i: (b, i, p)),
        out_shape=jax.ShapeDtypeStruct((nb, s, D_MODEL), BF16),
        compiler_params=_params(("parallel", "parallel", "arbitrary")),
        name="attn",
    )(qa, ka, vt, sel)


def _qk_s_kernel(q_ref, k_ref, cos_ref, sin_ref, qw_ref, kw_ref, qr_ref, kr_ref):
    rows = q_ref.shape[0]
    cos = cos_ref[...]
    sin = sin_ref[...]
    bd, first = _head_consts(rows)
    for cb in range(D_MODEL // LANES):
        sl = slice(cb * LANES, (cb + 1) * LANES)
        qr_ref[:, sl] = _head_norm_rope(q_ref[:, sl], qw_ref[...], cos, sin, bd, first)
        kr_ref[:, sl] = _head_norm_rope(k_ref[:, sl], kw_ref[...], cos, sin, bd, first)


def _qk_s(p2, cos, sin, qw, kw):
    n = p2.shape[0]
    col = lambda cidx: pl.BlockSpec((n, D_MODEL), lambda i: (0, cidx))
    full = lambda r: pl.BlockSpec((r, LANES), lambda i: (0, 0))
    return pl.pallas_call(
        _qk_s_kernel,
        grid=(1,),
        in_specs=[col(COL_Q), col(COL_K), full(n), full(n), full(1), full(1)],
        out_specs=[pl.BlockSpec((n, D_MODEL), lambda i: (0, 0))] * 2,
        out_shape=[jax.ShapeDtypeStruct((n, D_MODEL), F32)] * 2,
        compiler_params=_params(("arbitrary",)),
        name="qk_s",
    )(p2, p2, cos, sin, qw, kw)


def _dec_gate_kernel(pt_ref, q_ref, *refs, nblk, bps):
    del pt_ref
    k_refs = refs[:2 * bps]
    s_ref, sel_ref, qb_scr, gate_scr = refs[2 * bps:]
    j = pl.program_id(1)
    lane = _iota((ATT_HEADS, LANES), 1)

    @pl.when(j == 0)
    def _():
        qb_scr[...] = jnp.broadcast_to(q_ref[0], qb_scr.shape)
        gate_scr[...] = jnp.full_like(gate_scr, NEG_INF)

    for u in range(bps):
        for h in range(ATT_HEADS):
            qh = qb_scr[h]
            s_ref[0, u, h:h + 1, 0:PAGE_SIZE] = jnp.sum(k_refs[2 * u][h] * qh, axis=0, keepdims=True)
            s_ref[0, u, h:h + 1, PAGE_SIZE:] = jnp.sum(k_refs[2 * u + 1][h] * qh, axis=0, keepdims=True)
        s = s_ref[0, u]
        g = jnp.sum(s, axis=-1, keepdims=True) * (1.0 / MOBA_BLOCK)
        gate_scr[...] = jnp.where(lane == j * bps + u, g, gate_scr[...])
        s_ref[0, u] = s * (ATT_HEADDIM ** -0.5)

    @pl.when(j == pl.num_programs(1) - 1)
    def _():
        g = gate_scr[...]
        chosen = jnp.zeros((ATT_HEADS, LANES), F32)
        for _ in range(MOBA_TOPK):
            m = jnp.max(g, axis=-1, keepdims=True)
            idx = jnp.min(jnp.where(g == m, lane, LANES), axis=-1, keepdims=True)
            pick = lane == idx
            chosen = jnp.where(pick & (lane < nblk), 1.0, chosen)
            g = jnp.where(pick, NEG_INF, g)
        sel_ref[0] = chosen


def _dec_attend_kernel(pt_ref, blist_ref, nsel_ref, sel_ref, q_ref, kn_ref, vn_ref, *refs, bps):
    del pt_ref
    s_refs = refs[:bps]
    v_refs = refs[bps:3 * bps]
    o_ref, m_scr, l_scr, acc_scr = refs[3 * bps:]
    b = pl.program_id(0)
    j = pl.program_id(1)

    @pl.when(j == 0)
    def _():
        for h in range(ATT_HEADS):
            m_scr[h:h + 1, :] = jnp.sum(q_ref[0, h] * kn_ref[0, h], axis=0, keepdims=True) * (ATT_HEADDIM ** -0.5)
        l_scr[...] = jnp.ones_like(l_scr)
        acc_scr[...] = vn_ref[0]

    for u in range(bps):
        k = j * bps + u

        @pl.when(k < nsel_ref[b])
        def _(u=u, k=k):
            lane = _iota((ATT_HEADS, LANES), 1)
            on = jnp.sum(jnp.where(lane == blist_ref[b, k], sel_ref[0], 0.0), axis=-1, keepdims=True) > 0.5
            s = jnp.where(on, s_refs[u][0, 0], NEG_INF)
            m = m_scr[...]
            m_new = jnp.maximum(m, jnp.max(s, axis=-1, keepdims=True))
            p = jnp.exp(s - m_new)
            alpha = jnp.exp(m - m_new)
            m_scr[...] = m_new
            l_scr[...] = alpha * l_scr[...] + jnp.sum(p, axis=-1, keepdims=True)
            v0_ref, v1_ref = v_refs[2 * u], v_refs[2 * u + 1]
            for h in range(ATT_HEADS):
                pv = jnp.sum(v0_ref[h] * p[h:h + 1, 0:PAGE_SIZE] + v1_ref[h] * p[h:h + 1, PAGE_SIZE:],
                             axis=-1, keepdims=True)
                acc_scr[h] = alpha[h:h + 1, :] * acc_scr[h] + pv

    @pl.when(j == pl.num_programs(1) - 1)
    def _():
        for h in range(ATT_HEADS):
            o_ref[0, h] = acc_scr[h] / l_scr[h:h + 1, :]


def _dec_attn(q_rot, k_new, v_new, pool_k, pool_v, page_table):
    nb = q_rot.shape[0]
    n_pages = page_table.shape[1]
    per = MOBA_BLOCK // PAGE_SIZE
    assert per == 2
    nblk = n_pages // per
    assert MOBA_TOPK <= nblk <= LANES
    col_shape = (1, ATT_HEADS, ATT_HEADDIM, 1)
    page_shape = (None, ATT_HEADS, ATT_HEADDIM, PAGE_SIZE)
    s_shape = (1, 1, ATT_HEADS, MOBA_BLOCK)

    bps = DEC_BLOCKS_PER_STEP
    assert nblk % bps == 0
    k_page = lambda r: pl.BlockSpec(page_shape, lambda b, j, pt: (pt[b, per * bps * j + r], 0, 0, 0))
    gate_spec = pltpu.PrefetchScalarGridSpec(
        num_scalar_prefetch=1,
        grid=(nb, nblk // bps),
        in_specs=[pl.BlockSpec(col_shape, lambda b, j, pt: (b, 0, 0, 0))] + [k_page(r) for r in range(per * bps)],
        out_specs=[pl.BlockSpec((1, bps, ATT_HEADS, MOBA_BLOCK), lambda b, j, pt: (b, j, 0, 0)),
                   pl.BlockSpec((1, ATT_HEADS, LANES), lambda b, j, pt: (b, 0, 0))],
        scratch_shapes=[pltpu.VMEM((ATT_HEADS, ATT_HEADDIM, PAGE_SIZE), F32), pltpu.VMEM((ATT_HEADS, LANES), F32)])
    scores, sel = pl.pallas_call(
        functools.partial(_dec_gate_kernel, nblk=nblk, bps=bps),
        grid_spec=gate_spec,
        out_shape=[jax.ShapeDtypeStruct((nb, nblk, ATT_HEADS, MOBA_BLOCK), F32),
                   jax.ShapeDtypeStruct((nb, ATT_HEADS, LANES), F32)],
        compiler_params=_params(("parallel", "arbitrary")),
        name="dec_gate",
    )(page_table, q_rot, *([pool_k] * (per * bps)))

    flag = (jnp.max(sel[:, :, :nblk], axis=1) > 0.5).astype(I32)
    nsel = jnp.sum(flag, axis=1)
    abps = DEC_ATTEND_BLOCKS_PER_STEP
    nsteps = -(-min(nblk, ATT_HEADS * MOBA_TOPK) // abps)
    order = jnp.argsort(1 - flag, axis=1, stable=True).astype(I32)
    slot = jnp.minimum(jnp.arange(nsteps * abps, dtype=I32)[None, :], (nsel - 1)[:, None])
    blist = jnp.take_along_axis(order, slot, axis=1)

    col = pl.BlockSpec(col_shape, lambda b, j, pt, bl, ns: (b, 0, 0, 0))
    s_blk = lambda u: pl.BlockSpec(s_shape, lambda b, j, pt, bl, ns: (b, bl[b, j * abps + u], 0, 0))
    v_page = lambda u, half: pl.BlockSpec(
        page_shape, lambda b, j, pt, bl, ns: (pt[b, per * bl[b, j * abps + u] + half], 0, 0, 0))

    attend_spec = pltpu.PrefetchScalarGridSpec(
        num_scalar_prefetch=3,
        grid=(nb, nsteps),
        in_specs=([pl.BlockSpec((1, ATT_HEADS, LANES), lambda b, j, pt, bl, ns: (b, 0, 0)), col, col, col]
                  + [s_blk(u) for u in range(abps)]
                  + [v_page(u, half) for u in range(abps) for half in range(per)]),
        out_specs=col,
        scratch_shapes=[pltpu.VMEM((ATT_HEADS, 1), F32), pltpu.VMEM((ATT_HEADS, 1), F32),
                        pltpu.VMEM((ATT_HEADS, ATT_HEADDIM, 1), F32)])
    return pl.pallas_call(
        functools.partial(_dec_attend_kernel, bps=abps),
        grid_spec=attend_spec,
        out_shape=jax.ShapeDtypeStruct((nb, ATT_HEADS, ATT_HEADDIM, 1), F32),
        compiler_params=_params(("parallel", "arbitrary")),
        name="dec_attend",
    )(page_table, blist, nsel, sel, q_rot, k_new, v_new, *([scores] * abps), *([pool_v] * (abps * per)))


def _headnorm_kernel(x_ref, w_ref, o_ref):
    for h in range(MEM_HEADS):
        sl = slice(h * MEM_HEADDIM, (h + 1) * MEM_HEADDIM)
        x = x_ref[:, sl]
        ms = jnp.mean(x * x, axis=-1, keepdims=True)
        o_ref[:, sl] = x * lax.rsqrt(ms + EPS) * w_ref[...]


def _headnorm(x2, w, tm):
    t = x2.shape[0]
    return pl.pallas_call(
        _headnorm_kernel,
        grid=(t // tm,),
        in_specs=[pl.BlockSpec((tm, D_MODEL), lambda i: (i, 0)), pl.BlockSpec((1, MEM_HEADDIM), lambda i: (0, 0))],
        out_specs=pl.BlockSpec((tm, D_MODEL), lambda i: (i, 0)),
        out_shape=jax.ShapeDtypeStruct((t, D_MODEL), F32),
        compiler_params=_params(("parallel",)),
        name="headnorm",
    )(x2, w)


def _mem_kernel(q_ref, k_ref, v_ref, qw_ref, o_ref):
    kb = k_ref[0].astype(BF16)
    vb = v_ref[0].astype(BF16)
    for h in range(MEM_HEADS):
        sl = slice(h * MEM_HEADDIM, (h + 1) * MEM_HEADDIM)
        q = q_ref[0, :, sl]
        ms = jnp.mean(q * q, axis=-1, keepdims=True)
        qn = (q * lax.rsqrt(ms + EPS) * qw_ref[...]).astype(BF16)
        s = _dot_nt(qn, kb[:, sl]) * (MEM_HEADDIM ** -0.5)
        e = jnp.exp(s - jnp.max(s, axis=-1, keepdims=True))
        p = e / jnp.sum(e, axis=-1, keepdims=True)
        o_ref[0, :, sl] = _dot(p.astype(BF16), vb[:, sl]).astype(BF16)


def _mem_attend(q3, qcol, mk, mv, qw, tq):
    nb, s, _ = q3.shape
    m = mk.shape[1]
    return pl.pallas_call(
        _mem_kernel,
        grid=(nb, s // tq),
        in_specs=[pl.BlockSpec((1, tq, D_MODEL), lambda b, i: (b, i, qcol)),
                  pl.BlockSpec((1, m, D_MODEL), lambda b, i: (b, 0, 0)),
                  pl.BlockSpec((1, m, D_MODEL), lambda b, i: (b, 0, 0)),
                  pl.BlockSpec((1, MEM_HEADDIM), lambda b, i: (0, 0))],
        out_specs=pl.BlockSpec((1, tq, D_MODEL), lambda b, i: (b, i, 0)),
        out_shape=jax.ShapeDtypeStruct((nb, s, D_MODEL), BF16),
        compiler_params=_params(("parallel", "parallel")),
        name="mem_attend",
    )(q3, mk, mv, qw)


def _merge_kernel(ys_ref, oa_ref, om_ref, g1_ref, g2_ref, g3_ref, x_ref, ws_ref, wa_ref, wm_ref, wo_ref, nw_ref,
                  x1_ref, hf_ref):
    merged = (_sigmoid(g1_ref[...]) * _dot(ys_ref[...], ws_ref[...])
              + _sigmoid(g2_ref[...]) * _dot(oa_ref[...], wa_ref[...])
              + _sigmoid(g3_ref[...]) * _dot(om_ref[...], wm_ref[...]))
    x1 = x_ref[...] + _dot(merged.astype(BF16), wo_ref[...])
    x1_ref[...] = x1
    ms = jnp.mean(x1 * x1, axis=-1, keepdims=True)
    hf_ref[...] = (x1 * lax.rsqrt(ms + EPS) * nw_ref[...]).astype(BF16)


def _merge(y_ssd, o_att, o_mem, p2, x2, ws, wa, wm, wo, nw, tm):
    t = x2.shape[0]
    tok = lambda w, c=0: pl.BlockSpec((tm, w), lambda i: (i, c))
    res = lambda a: pl.BlockSpec(a.shape, lambda i: (0, 0))
    return pl.pallas_call(
        _merge_kernel,
        grid=(t // tm,),
        in_specs=[tok(SSD_INNER), tok(D_MODEL), tok(D_MODEL),
                  tok(D_MODEL, COL_G), tok(D_MODEL, COL_G + 1), tok(D_MODEL, COL_G + 2), tok(D_MODEL),
                  res(ws), res(wa), res(wm), res(wo), res(nw)],
        out_specs=[tok(D_MODEL), tok(D_MODEL)],
        out_shape=[jax.ShapeDtypeStruct((t, D_MODEL), F32), jax.ShapeDtypeStruct((t, D_MODEL), BF16)],
        compiler_params=_params(("parallel",)),
        name="merge",
    )(y_ssd, o_att, o_mem, p2, p2, p2, x2, ws, wa, wm, wo, nw)


def _top16(srcs, n_rows, tp):
    r_id = _iota((n_rows, tp), 0)
    slot = _iota((PEER_TOPK, tp), 0)

    def body(r, carry):
        out = []
        for src_scr, (vals, idxs) in zip(srcs, carry):
            sc = src_scr[...]
            m = jnp.max(sc, axis=0, keepdims=True)
            idx = jnp.min(jnp.where(sc == m, r_id, n_rows), axis=0, keepdims=True)
            src_scr[...] = jnp.where(r_id == idx, NEG_INF, sc)
            out.append((jnp.where(slot == r, m, vals), jnp.where(slot == r, idx, idxs)))
        return tuple(out)

    zero = (jnp.zeros((PEER_TOPK, tp), F32), jnp.zeros((PEER_TOPK, tp), I32))
    return lax.fori_loop(0, PEER_TOPK, body, tuple(zero for _ in srcs))


def _peer_sel_kernel(hf_ref, wqt_ref, keys_ref, a_ref, b_ref, g_ref,
                     qt_scr, sc_scr, sv_scr, si_scr, cand_scr, oa_scr, ob_scr, og_scr):
    tp = hf_ref.shape[0]
    nk = PEER_NKEYS
    kk = PEER_TOPK
    qt_scr[...] = _dot_nt(wqt_ref[...], hf_ref[...])

    def halves(h, carry):
        for c in range(2):
            hc = 2 * h + c
            q_hc = qt_scr[pl.ds(pl.multiple_of(hc * nk, nk), nk), :]
            sc_scr[c] = _dot3(keys_ref[hc], q_hc, _dot)
        tops = _top16([sc_scr.at[0], sc_scr.at[1]], nk, tp)
        for c in range(2):
            sv_scr[2 * h + c] = tops[c][0]
            si_scr[2 * h + c] = tops[c][1]
        return carry

    lax.fori_loop(0, PEER_HEADS, halves, 0)

    def head_pair(hp, carry):
        b8 = _iota((SUBLANES, tp), 0)
        for w in range(2):
            sv1 = sv_scr[4 * hp + 2 * w]
            sv2 = sv_scr[4 * hp + 2 * w + 1]
            cand = cand_scr.at[w]
            cand[0:kk, :] = sv1[0:1, :] + sv2
            for a in range(1, SUBLANES):
                c = sv1[a:a + 1, :] + sv2[0:SUBLANES, :]
                n_b = kk // (a + 1)
                cand[SUBLANES * (a + 1):SUBLANES * (a + 2), :] = c if n_b >= SUBLANES else jnp.where(b8 < n_b, c, NEG_INF)
            cand[PEER_NCAND - SUBLANES:, :] = sv1[SUBLANES:, :] + sv2[0:1, :]
        tops = _top16([cand_scr.at[0], cand_scr.at[1]], PEER_NCAND, tp)
        for w in range(2):
            fv, fr = tops[w]
            si1 = si_scr[4 * hp + 2 * w]
            si2 = si_scr[4 * hp + 2 * w + 1]
            mid = fr - kk
            fa = jnp.where(fr < kk, 0,
                           jnp.where(fr < PEER_NCAND - SUBLANES, 1 + (mid >> 3), fr - (PEER_NCAND - 2 * SUBLANES)))
            fb = jnp.where(fr < kk, fr, jnp.where(fr < PEER_NCAND - SUBLANES, mid & (SUBLANES - 1), 0))
            e1 = jnp.zeros((kk, tp), I32)
            e2 = jnp.zeros((kk, tp), I32)
            for a in range(kk):
                e1 = e1 + jnp.where(fa == a, si1[a:a + 1, :], 0)
                e2 = e2 + jnp.where(fb == a, si2[a:a + 1, :], 0)
            e = jnp.exp(fv - fv[0:1, :])
            rows = pl.ds(pl.multiple_of((2 * hp + w) * kk, kk), kk)
            oa_scr[rows, :] = e1.astype(F32)
            ob_scr[rows, :] = e2.astype(F32)
            og_scr[rows, :] = e / jnp.sum(e, axis=0, keepdims=True)
        return carry

    lax.fori_loop(0, PEER_HEADS // 2, head_pair, 0)
    a_ref[...] = oa_scr[...].T
    b_ref[...] = ob_scr[...].T
    g_ref[...] = og_scr[...].T


def _peer_sel(hf, wqt, keys, tp):
    t = hf.shape[0]
    kk = PEER_TOPK
    npick = PEER_HEADS * kk
    out = pl.BlockSpec((tp, npick), lambda i: (i, 0))
    return pl.pallas_call(
        _peer_sel_kernel,
        grid=(t // tp,),
        in_specs=[pl.BlockSpec((tp, D_MODEL), lambda i: (i, 0)),
                  pl.BlockSpec(wqt.shape, lambda i: (0, 0)),
                  pl.BlockSpec(keys.shape, lambda i: (0, 0, 0))],
        out_specs=[out, out, out],
        out_shape=[jax.ShapeDtypeStruct((t, npick), F32)] * 3,
        scratch_shapes=[pltpu.VMEM((wqt.shape[0], tp), F32),
                        pltpu.VMEM((2, PEER_NKEYS, tp), F32),
                        pltpu.VMEM((2 * PEER_HEADS, kk, tp), F32),
                        pltpu.VMEM((2 * PEER_HEADS, kk, tp), I32),
                        pltpu.VMEM((2, PEER_NCAND, tp), F32),
                        pltpu.VMEM((npick, tp), F32),
                        pltpu.VMEM((npick, tp), F32),
                        pltpu.VMEM((npick, tp), F32)],
        compiler_params=_params(("parallel",)),
        name="peer_sel",
    )(hf, wqt, keys)


def _peer_kernel(hf_ref, a_ref, b_ref, g_ref, x1_ref, u_ref, v_ref, o_ref, g_scr, *, tm, te):
    nk = PEER_NKEYS
    e = pl.program_id(1)
    n1 = te // nk

    @pl.when(e == 0)
    def _():
        o_ref[...] = x1_ref[...]
        sub = _iota((nk, LANES), 0).astype(F32)

        def toks(tg, carry):
            for u in range(PEER_BUILD_TOKENS):
                t = tg * PEER_BUILD_TOKENS + u
                row = pl.ds(t, 1)
                y = jnp.where(sub == a_ref[row, :], g_ref[row, :], 0.0).astype(BF16)
                z = jnp.where(sub == b_ref[row, :], 1.0, 0.0).astype(BF16)
                g_scr[pl.ds(pl.multiple_of(t * nk, nk), nk), :] = _dot_nt(y, z)
            return carry

        lax.fori_loop(0, tm // PEER_BUILD_TOKENS, toks, 0)

    hf = hf_ref[...]
    ck = PEER_EXPERT_CHUNK
    for c in range(te // ck):
        act = _dot_nt(hf, u_ref[c * ck:(c + 1) * ck, :])
        act = 0.5 * act * (1.0 + lax.erf(act * (2.0 ** -0.5)))
        ws = []
        for k in range(ck // nk):
            gk = g_scr[pl.ds(e * n1 + c * (ck // nk) + k, tm, stride=nk), :]
            ws.append((act[:, k * nk:(k + 1) * nk] * gk).astype(BF16))
        o_ref[...] += _dot(jnp.concatenate(ws, axis=1), v_ref[c * ck:(c + 1) * ck, :])


def _peer(hf, a, b, g, x1, u, v, tm, te):
    t = hf.shape[0]
    n_exp = u.shape[0]
    npick = a.shape[1]
    tok = lambda w: pl.BlockSpec((tm, w), lambda i, e: (i, 0))
    exp = pl.BlockSpec((te, D_MODEL), lambda i, e: (e, 0))
    return pl.pallas_call(
        functools.partial(_peer_kernel, tm=tm, te=te),
        grid=(t // tm, n_exp // te),
        in_specs=[tok(D_MODEL), tok(npick), tok(npick), tok(npick), tok(D_MODEL), exp, exp],
        out_specs=tok(D_MODEL),
        out_shape=jax.ShapeDtypeStruct((t, D_MODEL), F32),
        scratch_shapes=[pltpu.VMEM((tm * PEER_NKEYS, PEER_NKEYS), F32)],
        compiler_params=_params(("parallel", "arbitrary")),
        name="peer",
    )(hf, a, b, g, x1, u, v)


def _pack_w_in(w_in):
    n0 = SSD_INNER + SSD_CONV_DIM
    w_main = jnp.concatenate([w_in[:, :n0], w_in[:, n0 + SSD_HEADS:]], axis=1).astype(BF16)
    w_dt = jnp.pad(w_in[:, n0:n0 + SSD_HEADS], ((0, 0), (0, LANES - SSD_HEADS))).astype(BF16)
    return w_main, w_dt


def _pad_lanes(v):
    return jnp.pad(v, (0, LANES - v.shape[0]))[None]


def kernel(x_prompt, x_sample, mem_prompt, cache_k, cache_v, page_table, cache_mem_k, cache_mem_v, state_ssm, state_conv, norm1_w, w_in, conv_w, conv_b, dt_bias, a_log, d_skip, ssd_norm_w, q_norm_w, k_norm_w, mem_norm_w, w_mem_k, w_mem_v, mem_qn_w, mem_kn_w, w_br_ssd, w_br_attn, w_br_mem, w_out, norm2_w, peer_wq, peer_keys, peer_u, peer_v):
    bp, sp, _ = x_prompt.shape
    bs, ss, _ = x_sample.shape
    assert ss == 1 and cache_k.shape[0] == 1
    tp, ts = bp * sp, bs * ss

    w_main, w_dt = _pack_w_in(w_in[0])
    nw1 = norm1_w[0][None]
    ssd_w = (conv_w[0], conv_b[0][None], _pad_lanes(dt_bias[0]), _pad_lanes(a_log[0]),
             jnp.repeat(d_skip[0], SSD_HEADDIM)[None], ssd_norm_w[0][None])
    qw2 = jnp.tile(q_norm_w[0], LANES // ATT_HEADDIM)[None]
    kw2 = jnp.tile(k_norm_w[0], LANES // ATT_HEADDIM)[None]
    w_mem_kv = jnp.concatenate([w_mem_k[0], w_mem_v[0]], axis=1).astype(BF16)
    merge_w = (w_br_ssd[0].astype(BF16), w_br_attn[0].astype(BF16), w_br_mem[0].astype(BF16),
               w_out[0].astype(BF16), norm2_w[0][None])
    wqt = peer_wq[0].T.astype(BF16)
    keys = peer_keys[0].reshape(2 * PEER_HEADS, PEER_NKEYS, -1)
    u_bf = peer_u[0].astype(BF16)
    v_bf = peer_v[0].astype(BF16)
    mem_qw = mem_qn_w[0][None]

    xp = x_prompt.reshape(tp, D_MODEL)
    p_main = _proj(xp, nw1, w_main, 1024, 1024)
    p_dt = _proj(xp, nw1, w_dt, 1024, LANES)
    p3 = p_main.reshape(bp, sp, N_MAIN)
    y_ssd, ssm_p = _ssd(p3, p_dt.reshape(bp, sp, LANES), jnp.zeros((bp, SUBLANES, SSD_CONV_DIM), F32),
                        jnp.zeros((bp, SSD_HEADS, SSD_HEADDIM, SSD_STATE), F32), *ssd_w, SSD_CHUNK)
    cos, sin = _rope_tables(jnp.arange(sp, dtype=I32))
    qa, k_rot, ka, v_p, vt, sel = _qk(p3, cos, sin, qw2, kw2)
    o_att = _attn(qa, ka, vt, sel)
    n_mem = mem_prompt.shape[1]
    kv = _proj(mem_prompt.reshape(bp * n_mem, D_MODEL), mem_norm_w[0][None], w_mem_kv, 256, D_MODEL)
    mem_k = _headnorm(kv, mem_kn_w[0][None], 256)
    mem_v = kv[:, D_MODEL:]
    o_mem = _mem_attend(p3, COL_MQ, mem_k.reshape(bp, n_mem, D_MODEL), mem_v.reshape(bp, n_mem, D_MODEL), mem_qw, 512)
    x1, hf = _merge(y_ssd.reshape(tp, SSD_INNER), o_att.reshape(tp, D_MODEL), o_mem.reshape(tp, D_MODEL),
                    p_main, xp, *merge_w, 512)
    pa, pb, pg = _peer_sel(hf, wqt, keys, 256)
    y_p = _peer(hf, pa, pb, pg, x1, u_bf, v_bf, 256, 2048)

    xs = x_sample.reshape(ts, D_MODEL)
    ps_main = _proj(xs, nw1, w_main, ts, 1024)
    ps_dt = _proj(xs, nw1, w_dt, ts, LANES)
    ps3 = ps_main.reshape(bs, 1, N_MAIN)
    conv_prev8 = jnp.pad(state_conv[0], ((0, 0), (SUBLANES - (SSD_CONV - 1), 0), (0, 0)))
    ys_ssd, ssm_s = _ssd(ps3, ps_dt.reshape(bs, 1, LANES), conv_prev8, state_ssm[0], *ssd_w, 1)
    cos_s, sin_s = _rope_tables(jnp.full((ts,), PAST_LEN, I32))
    q_rot_s, k_rot_s = _qk_s(ps_main, cos_s, sin_s, qw2, kw2)
    v_s = ps_main[:, COL_V * D_MODEL:(COL_V + 1) * D_MODEL]
    cols = lambda v: v.reshape(bs, ATT_HEADS, ATT_HEADDIM, 1)
    pool = lambda c: jnp.transpose(c[0], (0, 2, 3, 1))
    o_att_s = _dec_attn(cols(q_rot_s), cols(k_rot_s), cols(v_s), pool(cache_k), pool(cache_v), page_table)
    mq8 = jnp.broadcast_to(ps3[:, :, COL_MQ * D_MODEL:(COL_MQ + 1) * D_MODEL], (bs, SUBLANES, D_MODEL))
    o_mem_s = _mem_attend(mq8, 0, cache_mem_k[0].reshape(bs, n_mem, D_MODEL),
                          cache_mem_v[0].reshape(bs, n_mem, D_MODEL), mem_qw, SUBLANES)[:, 0]
    x1_s, hf_s = _merge(ys_ssd.reshape(ts, SSD_INNER), o_att_s.reshape(ts, D_MODEL).astype(BF16), o_mem_s,
                        ps_main, xs, *merge_w, ts)
    pad_rows = lambda v: jnp.pad(v, ((0, LANES - ts), (0, 0)))
    hf_sp = pad_rows(hf_s)
    sa, sb, sg = _peer_sel(hf_sp, wqt, keys, LANES)
    y_s = _peer(hf_sp, sa, sb, sg, pad_rows(x1_s), u_bf, v_bf, LANES, 2048)[:ts]

    kv_shape = (1, bp, sp, ATT_HEADS, ATT_HEADDIM)
    mem_shape = (1, bp, n_mem, MEM_HEADS, MEM_HEADDIM)
    xbc_cols = slice(COL_X * 1024, COL_X * 1024 + SSD_CONV_DIM)
    conv_p = p3[:, sp - (SSD_CONV - 1):, xbc_cols]
    conv_s = jnp.concatenate([state_conv[0][:, 1:], ps3[:, :, xbc_cols]], axis=1)
    return (y_p.reshape(bp, sp, D_MODEL), y_s.reshape(bs, ss, D_MODEL),
            k_rot.reshape(kv_shape), v_p.reshape(kv_shape),
            mem_k.reshape(mem_shape), mem_v.reshape(mem_shape),
            ssm_p[None], conv_p[None],
            k_rot_s.reshape(1, bs, ss, ATT_HEADS, ATT_HEADDIM), v_s.reshape(1, bs, ss, ATT_HEADS, ATT_HEADDIM),
            ssm_s[None], conv_s[None])
```

```python
import functools
import math

import jax
import jax.numpy as jnp
from jax import lax
from jax.experimental import pallas as pl
from jax.experimental.pallas import tpu as pltpu

F32 = jnp.float32
BF16 = jnp.bfloat16
I32 = jnp.int32
EPS = 1e-6
NEG_INF = float("-inf")

D_MODEL = 1024
SSD_INNER = 2048
SSD_HEADS = 32
SSD_HEADDIM = 64
SSD_GROUPS = 8
SSD_STATE = 128
SSD_CONV = 4
SSD_CHUNK = 128
SSD_CONV_DIM = 4096
ATT_HEADS = 16
ATT_HEADDIM = 64
MOBA_BLOCK = 256
MOBA_TOPK = 3
ROPE_THETA = 10000.0
MEM_HEADS = 4
MEM_HEADDIM = 256
PEER_HEADS = 8
PEER_NKEYS = 128
PEER_TOPK = 16
PEER_BUILD_TOKENS = 16
PEER_NCAND = 80
PAST_LEN = 16384
PAGE_SIZE = 128
DEC_BLOCKS_PER_STEP = 4
DEC_ATTEND_BLOCKS_PER_STEP = 2

LANES = 128
SUBLANES = 8
VMEM_LIMIT = 56 * 1024 * 1024

COL_Z, COL_X, COL_BC, COL_Q, COL_K, COL_V, COL_MQ, COL_G = 0, 2, 4, 6, 7, 8, 9, 10
N_MAIN = 13 * 1024


def _params(sem):
    return pltpu.CompilerParams(dimension_semantics=sem, vmem_limit_bytes=VMEM_LIMIT)


def _dot(a, b):
    return jnp.dot(a, b, preferred_element_type=F32)


def _dot_nt(a, b):
    return lax.dot_general(a, b, (((1,), (1,)), ((), ())), preferred_element_type=F32)


def _dot_tn(a, b):
    return lax.dot_general(a, b, (((0,), (0,)), ((), ())), preferred_element_type=F32)


def _split(a):
    hi = a.astype(BF16)
    lo = (a - hi.astype(F32)).astype(BF16)
    return hi, lo


def _dot3(a, b, dot):
    ah, al = _split(a)
    bh, bl = _split(b)
    return dot(ah, bh) + (dot(ah, bl) + dot(al, bh))


def _sigmoid(x):
    return 1.0 / (1.0 + jnp.exp(-x))


def _iota(shape, axis):
    return lax.broadcasted_iota(I32, shape, axis)


def _tree(x, op):
    while x.shape[0] > 1:
        h = x.shape[0] // 2
        x = op(x[:h], x[h:])
    return x[0]


def _proj_kernel(x_ref, nw_ref, w_ref, o_ref, h_scr):
    @pl.when(pl.program_id(1) == 0)
    def _():
        x = x_ref[...]
        ms = jnp.mean(x * x, axis=-1, keepdims=True)
        h_scr[...] = (x * lax.rsqrt(ms + EPS) * nw_ref[...]).astype(BF16)

    o_ref[...] = _dot(h_scr[...], w_ref[...])


def _proj(x, nw, w, tm, tn):
    t, d = x.shape
    n = w.shape[1]
    return pl.pallas_call(
        _proj_kernel,
        grid=(t // tm, n // tn),
        in_specs=[pl.BlockSpec((tm, d), lambda i, j: (i, 0)),
                  pl.BlockSpec((1, d), lambda i, j: (0, 0)),
                  pl.BlockSpec((d, tn), lambda i, j: (0, j))],
        out_specs=pl.BlockSpec((tm, tn), lambda i, j: (i, j)),
        out_shape=jax.ShapeDtypeStruct((t, n), F32),
        scratch_shapes=[pltpu.VMEM((tm, d), BF16)],
        compiler_params=_params(("parallel", "arbitrary")),
        name="proj",
    )(x, nw, w)


def _ssd_kernel(z_ref, x_ref, bc_ref, dt_ref, cprev_ref, h0_ref, cw_ref, cb_ref, dtb_ref, alog_ref,
                dsk_ref, nw_ref, y_ref, hl_ref, buf, st, ycat, *, rows):
    L = SSD_CHUNK
    c = pl.program_id(1)

    @pl.when(c == 0)
    def _():
        buf[0:SUBLANES, :] = cprev_ref[0]
        st[...] = h0_ref[0].reshape(st.shape)

    if rows < L:
        buf[SUBLANES:, :] = jnp.zeros((L, SSD_CONV_DIM), F32)
    buf[SUBLANES:SUBLANES + rows, 0:SSD_INNER] = x_ref[0]
    buf[SUBLANES:SUBLANES + rows, SSD_INNER:] = bc_ref[0]

    acc = buf[SUBLANES:SUBLANES + L, :] * cw_ref[SSD_CONV - 1:SSD_CONV, :]
    for j in range(SSD_CONV - 1):
        off = SUBLANES - (SSD_CONV - 1) + j
        acc = acc + buf[off:off + L, :] * cw_ref[j:j + 1, :]
    acc = acc + cb_ref[...]
    xbc = acc * _sigmoid(acc)
    buf[0:SUBLANES, :] = buf[L:L + SUBLANES, :]

    xs = xbc[:, :SSD_INNER]
    n_gn = SSD_GROUPS * SSD_STATE
    bm = xbc[:, SSD_INNER:SSD_INNER + n_gn].astype(BF16)
    cm = xbc[:, SSD_INNER + n_gn:].astype(BF16)

    dt_raw = dt_ref[0] if rows == L else jnp.broadcast_to(dt_ref[0], (L, LANES))
    pre = dt_raw + dtb_ref[...]
    dt = jnp.maximum(pre, 0.0) + jnp.log1p(jnp.exp(-jnp.abs(pre)))
    row_id = _iota((L, LANES), 0)
    dt = jnp.where(row_id < rows, dt, 0.0)
    a = -jnp.exp(alog_ref[...])
    da = dt * a
    tri = _iota((L, L), 0) >= _iota((L, L), 1)
    a_cs = jnp.dot(tri.astype(F32), da, preferred_element_type=F32, precision=lax.Precision.HIGHEST)
    a_cs_t = a_cs.T
    a_last = a_cs[L - 1:L, :]
    e_cs = jnp.exp(a_cs)
    d_end = jnp.exp(a_last - a_cs)
    e_last = jnp.exp(a_last)

    head_lanes = (_iota((LANES, SSD_INNER), 1) // SSD_HEADDIM == _iota((LANES, SSD_INNER), 0))
    head_lanes = jnp.where(head_lanes, 1.0, 0.0).astype(BF16)

    def spread(v):
        p1 = v.astype(BF16)
        r1 = v - p1.astype(F32)
        p2 = r1.astype(BF16)
        p3 = (r1 - p2.astype(F32)).astype(BF16)
        return _dot(p1, head_lanes) + (_dot(p2, head_lanes) + _dot(p3, head_lanes))

    xdt_all = xs * spread(dt)
    xdt_bf = xdt_all.astype(BF16)
    xw_bf = (xdt_all * spread(d_end)).astype(BF16)
    e_cs_all = spread(e_cs)
    left = _iota((L, LANES), 1) < SSD_HEADDIM
    top = _iota((LANES, SSD_STATE), 0) < SSD_HEADDIM

    for g in range(SSD_GROUPS):
        bg = bm[:, g * SSD_STATE:(g + 1) * SSD_STATE]
        cg = cm[:, g * SSD_STATE:(g + 1) * SSD_STATE]
        cb = _dot_nt(cg, bg)
        for r in range(SSD_HEADS // SSD_GROUPS // 2):
            pair = g * (SSD_HEADS // SSD_GROUPS // 2) + r
            ps = slice(pair * LANES, (pair + 1) * LANES)
            ms_ = []
            for h in (2 * pair, 2 * pair + 1):
                decay = jnp.where(tri, jnp.exp(a_cs[:, h:h + 1] - a_cs_t[h:h + 1, :]), 0.0)
                ms_.append((cb * decay).astype(BF16))
            xp = xdt_bf[:, ps]
            yd = jnp.where(left, _dot(ms_[0], xp), _dot(ms_[1], xp))
            hp = st[pair]
            yo = _dot_nt(cg, hp.astype(BF16)) * e_cs_all[:, ps]
            keep = jnp.where(top, e_last[:, 2 * pair:2 * pair + 1], e_last[:, 2 * pair + 1:2 * pair + 2])
            st[pair] = hp * keep + _dot_tn(xw_bf[:, ps], bg)
            ycat[:, ps] = yd + yo + xs[:, ps] * dsk_ref[:, ps]

    z = z_ref[0]
    if rows < L:
        y = ycat[0:rows, :] * (z * _sigmoid(z))
    else:
        y = ycat[...] * (z * _sigmoid(z))
    ms = jnp.mean(y * y, axis=-1, keepdims=True)
    y_ref[0] = (y * lax.rsqrt(ms + EPS) * nw_ref[...]).astype(BF16)

    @pl.when(c == pl.num_programs(1) - 1)
    def _():
        hl_ref[0] = st[...].reshape(SSD_HEADS, SSD_HEADDIM, SSD_STATE)


def _ssd(p3, dt3, conv_prev8, h0, cw, cb, dtb, alog, dsk, nw, rows):
    nb, s, _ = p3.shape
    nc = s // rows
    w2 = SSD_INNER
    const = lambda shape: pl.BlockSpec(shape, lambda b, c: tuple(0 for _ in shape))
    return pl.pallas_call(
        functools.partial(_ssd_kernel, rows=rows),
        grid=(nb, nc),
        in_specs=[pl.BlockSpec((1, rows, w2), lambda b, c: (b, c, COL_Z // 2)),
                  pl.BlockSpec((1, rows, w2), lambda b, c: (b, c, COL_X // 2)),
                  pl.BlockSpec((1, rows, w2), lambda b, c: (b, c, COL_BC // 2)),
                  pl.BlockSpec((1, rows, LANES), lambda b, c: (b, c, 0)),
                  pl.BlockSpec((1, SUBLANES, SSD_CONV_DIM), lambda b, c: (b, 0, 0)),
                  pl.BlockSpec((1, SSD_HEADS, SSD_HEADDIM, SSD_STATE), lambda b, c: (b, 0, 0, 0)),
                  const((SSD_CONV, SSD_CONV_DIM)), const((1, SSD_CONV_DIM)), const((1, LANES)),
                  const((1, LANES)), const((1, SSD_INNER)), const((1, SSD_INNER))],
        out_specs=[pl.BlockSpec((1, rows, w2), lambda b, c: (b, c, 0)),
                   pl.BlockSpec((1, SSD_HEADS, SSD_HEADDIM, SSD_STATE), lambda b, c: (b, 0, 0, 0))],
        out_shape=[jax.ShapeDtypeStruct((nb, s, w2), BF16),
                   jax.ShapeDtypeStruct((nb, SSD_HEADS, SSD_HEADDIM, SSD_STATE), F32)],
        scratch_shapes=[pltpu.VMEM((SUBLANES + SSD_CHUNK, SSD_CONV_DIM), F32),
                        pltpu.VMEM((SSD_HEADS // 2, 2 * SSD_HEADDIM, SSD_STATE), F32),
                        pltpu.VMEM((SSD_CHUNK, SSD_INNER), F32)],
        compiler_params=_params(("parallel", "arbitrary")),
        name="ssd",
    )(p3, p3, p3, dt3, conv_prev8, h0, cw, cb, dtb, alog, dsk, nw)


def _rope_tables(pos):
    d = ATT_HEADDIM
    inv = 1.0 / (ROPE_THETA ** (jnp.arange(0, d, 2, dtype=F32) / d))
    ang = pos.astype(F32)[:, None] * inv[None, :]
    cos = jnp.cos(ang)
    sin = jnp.sin(ang)
    cos = jnp.concatenate([cos, cos, cos, cos], axis=-1)
    sin = jnp.concatenate([-sin, sin, -sin, sin], axis=-1)
    return cos, sin


def _head_norm_rope(x, w, cos, sin, bd, first):
    hi, lo = _split(x * x)
    ms = _dot(hi, bd) + _dot(lo, bd)
    xn = x * lax.rsqrt(ms + EPS) * w
    partner = jnp.where(first, pltpu.roll(xn, LANES - ATT_HEADDIM // 2, 1), pltpu.roll(xn, ATT_HEADDIM // 2, 1))
    return xn * cos + partner * sin


def _head_consts(rows):
    r = _iota((LANES, LANES), 0) // ATT_HEADDIM
    c = _iota((LANES, LANES), 1) // ATT_HEADDIM
    bd = jnp.where(r == c, 1.0 / ATT_HEADDIM, 0.0).astype(BF16)
    first = (_iota((rows, LANES), 1) % ATT_HEADDIM) < (ATT_HEADDIM // 2)
    return bd, first


def _qk_kernel(q_ref, k_ref, v_ref, cos_ref, sin_ref, qw_ref, kw_ref,
               qa_ref, kr_ref, ka_ref, vo_ref, vt_ref, sel_ref, km_scr, qr_scr, *, nblk):
    tb = MOBA_BLOCK
    i = pl.program_id(1)

    @pl.when(i == 0)
    def _():
        km_scr[...] = jnp.zeros_like(km_scr)

    cos = cos_ref[...]
    sin = sin_ref[...]
    bd, first = _head_consts(tb)
    for cb in range(D_MODEL // LANES):
        sl = slice(cb * LANES, (cb + 1) * LANES)
        qr = _head_norm_rope(q_ref[0, :, sl], qw_ref[...], cos, sin, bd, first)
        kr = _head_norm_rope(k_ref[0, :, sl], kw_ref[...], cos, sin, bd, first)
        qr_scr[:, sl] = qr
        qa_ref[0, :, sl] = (qr * (ATT_HEADDIM ** -0.5)).astype(BF16)
        kr_ref[0, :, sl] = kr
        ka_ref[0, :, sl] = kr.astype(BF16)
        k_mean = jnp.sum(kr, axis=0, keepdims=True) * (1.0 / tb)
        km_scr[:, sl] = jnp.where(_iota((nblk, LANES), 0) == i, k_mean, km_scr[:, sl])
    v = v_ref[0]
    vo_ref[0] = v
    vt_ref[0, 0] = v.T.astype(BF16)

    km = km_scr[...]
    lane_head = _iota((nblk, D_MODEL), 1) // ATT_HEADDIM
    kmt = jnp.concatenate([jnp.where(lane_head == h, km, 0.0) for h in range(ATT_HEADS)], axis=0)
    gate = _dot3(kmt, qr_scr[...], _dot_nt)
    n_id = _iota((nblk, tb), 0)
    valid = n_id < i
    for h in range(ATT_HEADS):
        g = jnp.where(valid, gate[h * nblk:(h + 1) * nblk, :], NEG_INF)
        chosen = jnp.zeros((nblk, tb), F32)
        for _ in range(MOBA_TOPK):
            m = jnp.max(g, axis=0, keepdims=True)
            idx = jnp.min(jnp.where(g == m, n_id, nblk), axis=0, keepdims=True)
            pick = n_id == idx
            chosen = jnp.where(pick & valid, 1.0, chosen)
            g = jnp.where(pick, NEG_INF, g)
        sel_ref[0, h] = chosen


def _qk(p3, cos, sin, qw, kw):
    nb, s, _ = p3.shape
    tb = MOBA_BLOCK
    nblk = s // tb
    col = lambda cidx: pl.BlockSpec((1, tb, D_MODEL), lambda b, i: (b, i, cidx))
    out_tok = pl.BlockSpec((1, tb, D_MODEL), lambda b, i: (b, i, 0))
    return pl.pallas_call(
        functools.partial(_qk_kernel, nblk=nblk),
        grid=(nb, nblk),
        in_specs=[col(COL_Q), col(COL_K), col(COL_V),
                  pl.BlockSpec((tb, LANES), lambda b, i: (i, 0)),
                  pl.BlockSpec((tb, LANES), lambda b, i: (i, 0)),
                  pl.BlockSpec((1, LANES), lambda b, i: (0, 0)),
                  pl.BlockSpec((1, LANES), lambda b, i: (0, 0))],
        out_specs=[out_tok, out_tok, out_tok, out_tok,
                   pl.BlockSpec((1, 1, D_MODEL, tb), lambda b, i: (b, i, 0, 0)),
                   pl.BlockSpec((1, ATT_HEADS, nblk, tb), lambda b, i: (b, 0, 0, i))],
        out_shape=[jax.ShapeDtypeStruct((nb, s, D_MODEL), BF16),
                   jax.ShapeDtypeStruct((nb, s, D_MODEL), F32),
                   jax.ShapeDtypeStruct((nb, s, D_MODEL), BF16),
                   jax.ShapeDtypeStruct((nb, s, D_MODEL), F32),
                   jax.ShapeDtypeStruct((nb, nblk, D_MODEL, tb), BF16),
                   jax.ShapeDtypeStruct((nb, ATT_HEADS, nblk, s), F32)],
        scratch_shapes=[pltpu.VMEM((nblk, D_MODEL), F32), pltpu.VMEM((tb, D_MODEL), F32)],
        compiler_params=_params(("parallel", "arbitrary")),
        name="qk",
    )(p3, p3, p3, cos, sin, qw, kw)


def _attn_kernel(q_ref, k_ref, vt_ref, sel_ref, o_ref):
    tb = MOBA_BLOCK
    hd = ATT_HEADDIM
    i = pl.program_id(2)
    nh = LANES // hd
    q = q_ref[0]
    lane_head = _iota((tb, LANES), 1) // hd
    causal = _iota((tb, tb), 0) <= _iota((tb, tb), 1)
    qhs = [jnp.where(lane_head == hh, q, jnp.zeros_like(q)) for hh in range(nh)]

    ones_rows = jnp.ones((2 * SUBLANES, tb), BF16)

    def v_ones(j, hh):
        return jnp.concatenate([vt_ref[0, j, hh * hd:(hh + 1) * hd, :], ones_rows], axis=0)

    n_trips = i // 2 + 1

    def blocks(jj):
        return jnp.where(jj == 0, i, 2 * jj - 1), 2 * jj

    def scores(jj):
        kj = jnp.concatenate([k_ref[0, pl.ds(pl.multiple_of(b * tb, tb), tb), :] for b in blocks(jj)], axis=0)
        return [_dot_nt(kj, qhs[hh]) for hh in range(nh)]

    def trip(jj, carry, first):
        ss = carry[2 * nh:]
        nxt = scores(jnp.minimum(jj + 1, n_trips - 1))
        new = []
        for hh in range(nh):
            m, acc = carry[2 * hh:2 * hh + 2]
            for u, blk in enumerate(blocks(jj)):
                s = ss[hh][u * tb:(u + 1) * tb]
                if first and u == 0:
                    s = jnp.where(causal, s, NEG_INF)
                    smax = jnp.max(s, axis=0, keepdims=True)
                else:
                    on = sel_ref[0, hh, pl.ds(blk, 1), :] > 0.5
                    smax = jnp.where(on, jnp.max(s, axis=0, keepdims=True), NEG_INF)
                m_new = jnp.maximum(m, smax)
                p = jnp.exp(s - m_new)
                pv = _dot(v_ones(blk, hh), p.astype(BF16))
                if first and u == 0:
                    acc = pv
                else:
                    acc = acc * jnp.exp(m - m_new) + jnp.where(on, pv, 0.0)
                m = m_new
            new += [m, acc]
        return tuple(new + nxt)

    state = []
    for hh in range(nh):
        state += [jnp.full((1, tb), NEG_INF, F32), jnp.zeros((hd + 2 * SUBLANES, tb), F32)]
    state = trip(0, tuple(state + scores(0)), True)
    state = lax.fori_loop(1, n_trips, lambda jj, c: trip(jj, c, False), state)
    outs = [state[2 * hh + 1][:hd] / state[2 * hh + 1][hd:hd + 1] for hh in range(nh)]
    o_ref[0] = jnp.concatenate(outs, axis=0).T.astype(BF16)


def _attn(qa, ka, vt, sel):
    nb, s, _ = qa.shape
    tb = MOBA_BLOCK
    nblk = s // tb
    npair = D_MODEL // LANES
    hp = LANES // ATT_HEADDIM
    return pl.pallas_call(
        _attn_kernel,
        grid=(nb, npair, nblk),
        in_specs=[pl.BlockSpec((1, tb, LANES), lambda b, p, i: (b, i, p)),
                  pl.BlockSpec((1, s, LANES), lambda b, p, i: (b, 0, p)),
                  pl.BlockSpec((1, nblk, LANES, tb), lambda b, p, i: (b, 0, p, 0)),
                  pl.BlockSpec((1, hp, nblk, tb), lambda b, p, i: (b, p, 0, i))],
        out_specs=pl.BlockSpec((1, tb, LANES), lambda b, p, i: (b, i, p)),
        out_shape=jax.ShapeDtypeStruct((nb, s, D_MODEL), BF16),
        compiler_params=_params(("parallel", "parallel", "arbitrary")),
        name="attn",
    )(qa, ka, vt, sel)


def _qk_s_kernel(q_ref, k_ref, cos_ref, sin_ref, qw_ref, kw_ref, qr_ref, kr_ref):
    rows = q_ref.shape[0]
    cos = cos_ref[...]
    sin = sin_ref[...]
    bd, first = _head_consts(rows)
    for cb in range(D_MODEL // LANES):
        sl = slice(cb * LANES, (cb + 1) * LANES)
        qr_ref[:, sl] = _head_norm_rope(q_ref[:, sl], qw_ref[...], cos, sin, bd, first)
        kr_ref[:, sl] = _head_norm_rope(k_ref[:, sl], kw_ref[...], cos, sin, bd, first)


def _qk_s(p2, cos, sin, qw, kw):
    n = p2.shape[0]
    col = lambda cidx: pl.BlockSpec((n, D_MODEL), lambda i: (0, cidx))
    full = lambda r: pl.BlockSpec((r, LANES), lambda i: (0, 0))
    return pl.pallas_call(
        _qk_s_kernel,
        grid=(1,),
        in_specs=[col(COL_Q), col(COL_K), full(n), full(n), full(1), full(1)],
        out_specs=[pl.BlockSpec((n, D_MODEL), lambda i: (0, 0))] * 2,
        out_shape=[jax.ShapeDtypeStruct((n, D_MODEL), F32)] * 2,
        compiler_params=_params(("arbitrary",)),
        name="qk_s",
    )(p2, p2, cos, sin, qw, kw)


def _dec_gate_kernel(pt_ref, q_ref, *refs, nblk, bps):
    del pt_ref
    k_refs = refs[:2 * bps]
    s_ref, sel_ref, qb_scr, gate_scr = refs[2 * bps:]
    j = pl.program_id(1)
    lane = _iota((ATT_HEADS, LANES), 1)

    @pl.when(j == 0)
    def _():
        qb_scr[...] = jnp.broadcast_to(q_ref[0], qb_scr.shape)
        gate_scr[...] = jnp.full_like(gate_scr, NEG_INF)

    for u in range(bps):
        for h in range(ATT_HEADS):
            qh = qb_scr[h]
            s_ref[0, u, h:h + 1, 0:PAGE_SIZE] = jnp.sum(k_refs[2 * u][h] * qh, axis=0, keepdims=True)
            s_ref[0, u, h:h + 1, PAGE_SIZE:] = jnp.sum(k_refs[2 * u + 1][h] * qh, axis=0, keepdims=True)
        s = s_ref[0, u]
        g = jnp.sum(s, axis=-1, keepdims=True) * (1.0 / MOBA_BLOCK)
        gate_scr[...] = jnp.where(lane == j * bps + u, g, gate_scr[...])
        s_ref[0, u] = s * (ATT_HEADDIM ** -0.5)

    @pl.when(j == pl.num_programs(1) - 1)
    def _():
        g = gate_scr[...]
        chosen = jnp.zeros((ATT_HEADS, LANES), F32)
        for _ in range(MOBA_TOPK):
            m = jnp.max(g, axis=-1, keepdims=True)
            idx = jnp.min(jnp.where(g == m, lane, LANES), axis=-1, keepdims=True)
            pick = lane == idx
            chosen = jnp.where(pick & (lane < nblk), 1.0, chosen)
            g = jnp.where(pick, NEG_INF, g)
        sel_ref[0] = chosen


def _dec_attend_kernel(pt_ref, blist_ref, nsel_ref, sel_ref, q_ref, kn_ref, vn_ref, *refs, bps):
    del pt_ref
    s_refs = refs[:bps]
    v_refs = refs[bps:3 * bps]
    o_ref, m_scr, l_scr, acc_scr = refs[3 * bps:]
    b = pl.program_id(0)
    j = pl.program_id(1)

    @pl.when(j == 0)
    def _():
        for h in range(ATT_HEADS):
            m_scr[h:h + 1, :] = jnp.sum(q_ref[0, h] * kn_ref[0, h], axis=0, keepdims=True) * (ATT_HEADDIM ** -0.5)
        l_scr[...] = jnp.ones_like(l_scr)
        acc_scr[...] = vn_ref[0]

    for u in range(bps):
        k = j * bps + u

        @pl.when(k < nsel_ref[b])
        def _(u=u, k=k):
            lane = _iota((ATT_HEADS, LANES), 1)
            on = jnp.sum(jnp.where(lane == blist_ref[b, k], sel_ref[0], 0.0), axis=-1, keepdims=True) > 0.5
            s = jnp.where(on, s_refs[u][0, 0], NEG_INF)
            m = m_scr[...]
            m_new = jnp.maximum(m, jnp.max(s, axis=-1, keepdims=True))
            p = jnp.exp(s - m_new)
            alpha = jnp.exp(m - m_new)
            m_scr[...] = m_new
            l_scr[...] = alpha * l_scr[...] + jnp.sum(p, axis=-1, keepdims=True)
            v0_ref, v1_ref = v_refs[2 * u], v_refs[2 * u + 1]
            for h in range(ATT_HEADS):
                pv = jnp.sum(v0_ref[h] * p[h:h + 1, 0:PAGE_SIZE] + v1_ref[h] * p[h:h + 1, PAGE_SIZE:],
                             axis=-1, keepdims=True)
                acc_scr[h] = alpha[h:h + 1, :] * acc_scr[h] + pv

    @pl.when(j == pl.num_programs(1) - 1)
    def _():
        for h in range(ATT_HEADS):
            o_ref[0, h] = acc_scr[h] / l_scr[h:h + 1, :]


def _dec_attn(q_rot, k_new, v_new, pool_k, pool_v, page_table):
    nb = q_rot.shape[0]
    n_pages = page_table.shape[1]
    per = MOBA_BLOCK // PAGE_SIZE
    assert per == 2
    nblk = n_pages // per
    assert MOBA_TOPK <= nblk <= LANES
    col_shape = (1, ATT_HEADS, ATT_HEADDIM, 1)
    page_shape = (None, ATT_HEADS, ATT_HEADDIM, PAGE_SIZE)
    s_shape = (1, 1, ATT_HEADS, MOBA_BLOCK)

    bps = DEC_BLOCKS_PER_STEP
    assert nblk % bps == 0
    k_page = lambda r: pl.BlockSpec(page_shape, lambda b, j, pt: (pt[b, per * bps * j + r], 0, 0, 0))
    gate_spec = pltpu.PrefetchScalarGridSpec(
        num_scalar_prefetch=1,
        grid=(nb, nblk // bps),
        in_specs=[pl.BlockSpec(col_shape, lambda b, j, pt: (b, 0, 0, 0))] + [k_page(r) for r in range(per * bps)],
        out_specs=[pl.BlockSpec((1, bps, ATT_HEADS, MOBA_BLOCK), lambda b, j, pt: (b, j, 0, 0)),
                   pl.BlockSpec((1, ATT_HEADS, LANES), lambda b, j, pt: (b, 0, 0))],
        scratch_shapes=[pltpu.VMEM((ATT_HEADS, ATT_HEADDIM, PAGE_SIZE), F32), pltpu.VMEM((ATT_HEADS, LANES), F32)])
    scores, sel = pl.pallas_call(
        functools.partial(_dec_gate_kernel, nblk=nblk, bps=bps),
        grid_spec=gate_spec,
        out_shape=[jax.ShapeDtypeStruct((nb, nblk, ATT_HEADS, MOBA_BLOCK), F32),
                   jax.ShapeDtypeStruct((nb, ATT_HEADS, LANES), F32)],
        compiler_params=_params(("parallel", "arbitrary")),
        name="dec_gate",
    )(page_table, q_rot, *([pool_k] * (per * bps)))

    flag = (jnp.max(sel[:, :, :nblk], axis=1) > 0.5).astype(I32)
    nsel = jnp.sum(flag, axis=1)
    abps = DEC_ATTEND_BLOCKS_PER_STEP
    nsteps = -(-min(nblk, ATT_HEADS * MOBA_TOPK) // abps)
    order = jnp.argsort(1 - flag, axis=1, stable=True).astype(I32)
    slot = jnp.minimum(jnp.arange(nsteps * abps, dtype=I32)[None, :], (nsel - 1)[:, None])
    blist = jnp.take_along_axis(order, slot, axis=1)

    col = pl.BlockSpec(col_shape, lambda b, j, pt, bl, ns: (b, 0, 0, 0))
    s_blk = lambda u: pl.BlockSpec(s_shape, lambda b, j, pt, bl, ns: (b, bl[b, j * abps + u], 0, 0))
    v_page = lambda u, half: pl.BlockSpec(
        page_shape, lambda b, j, pt, bl, ns: (pt[b, per * bl[b, j * abps + u] + half], 0, 0, 0))

    attend_spec = pltpu.PrefetchScalarGridSpec(
        num_scalar_prefetch=3,
        grid=(nb, nsteps),
        in_specs=([pl.BlockSpec((1, ATT_HEADS, LANES), lambda b, j, pt, bl, ns: (b, 0, 0)), col, col, col]
                  + [s_blk(u) for u in range(abps)]
                  + [v_page(u, half) for u in range(abps) for half in range(per)]),
        out_specs=col,
        scratch_shapes=[pltpu.VMEM((ATT_HEADS, 1), F32), pltpu.VMEM((ATT_HEADS, 1), F32),
                        pltpu.VMEM((ATT_HEADS, ATT_HEADDIM, 1), F32)])
    return pl.pallas_call(
        functools.partial(_dec_attend_kernel, bps=abps),
        grid_spec=attend_spec,
        out_shape=jax.ShapeDtypeStruct((nb, ATT_HEADS, ATT_HEADDIM, 1), F32),
        compiler_params=_params(("parallel", "arbitrary")),
        name="dec_attend",
    )(page_table, blist, nsel, sel, q_rot, k_new, v_new, *([scores] * abps), *([pool_v] * (abps * per)))


def _headnorm_kernel(x_ref, w_ref, o_ref):
    for h in range(MEM_HEADS):
        sl = slice(h * MEM_HEADDIM, (h + 1) * MEM_HEADDIM)
        x = x_ref[:, sl]
        ms = jnp.mean(x * x, axis=-1, keepdims=True)
        o_ref[:, sl] = x * lax.rsqrt(ms + EPS) * w_ref[...]


def _headnorm(x2, w, tm):
    t = x2.shape[0]
    return pl.pallas_call(
        _headnorm_kernel,
        grid=(t // tm,),
        in_specs=[pl.BlockSpec((tm, D_MODEL), lambda i: (i, 0)), pl.BlockSpec((1, MEM_HEADDIM), lambda i: (0, 0))],
        out_specs=pl.BlockSpec((tm, D_MODEL), lambda i: (i, 0)),
        out_shape=jax.ShapeDtypeStruct((t, D_MODEL), F32),
        compiler_params=_params(("parallel",)),
        name="headnorm",
    )(x2, w)


def _mem_kernel(q_ref, k_ref, v_ref, qw_ref, o_ref):
    kb = k_ref[0].astype(BF16)
    vb = v_ref[0].astype(BF16)
    for h in range(MEM_HEADS):
        sl = slice(h * MEM_HEADDIM, (h + 1) * MEM_HEADDIM)
        q = q_ref[0, :, sl]
        ms = jnp.mean(q * q, axis=-1, keepdims=True)
        qn = (q * lax.rsqrt(ms + EPS) * qw_ref[...]).astype(BF16)
        s = _dot_nt(qn, kb[:, sl]) * (MEM_HEADDIM ** -0.5)
        e = jnp.exp(s - jnp.max(s, axis=-1, keepdims=True))
        p = e / jnp.sum(e, axis=-1, keepdims=True)
        o_ref[0, :, sl] = _dot(p.astype(BF16), vb[:, sl]).astype(BF16)


def _mem_attend(q3, qcol, mk, mv, qw, tq):
    nb, s, _ = q3.shape
    m = mk.shape[1]
    return pl.pallas_call(
        _mem_kernel,
        grid=(nb, s // tq),
        in_specs=[pl.BlockSpec((1, tq, D_MODEL), lambda b, i: (b, i, qcol)),
                  pl.BlockSpec((1, m, D_MODEL), lambda b, i: (b, 0, 0)),
                  pl.BlockSpec((1, m, D_MODEL), lambda b, i: (b, 0, 0)),
                  pl.BlockSpec((1, MEM_HEADDIM), lambda b, i: (0, 0))],
        out_specs=pl.BlockSpec((1, tq, D_MODEL), lambda b, i: (b, i, 0)),
        out_shape=jax.ShapeDtypeStruct((nb, s, D_MODEL), BF16),
        compiler_params=_params(("parallel", "parallel")),
        name="mem_attend",
    )(q3, mk, mv, qw)


def _merge_kernel(ys_ref, oa_ref, om_ref, g1_ref, g2_ref, g3_ref, x_ref, ws_ref, wa_ref, wm_ref, wo_ref, nw_ref,
                  x1_ref, hf_ref):
    merged = (_sigmoid(g1_ref[...]) * _dot(ys_ref[...], ws_ref[...])
              + _sigmoid(g2_ref[...]) * _dot(oa_ref[...], wa_ref[...])
              + _sigmoid(g3_ref[...]) * _dot(om_ref[...], wm_ref[...]))
    x1 = x_ref[...] + _dot(merged.astype(BF16), wo_ref[...])
    x1_ref[...] = x1
    ms = jnp.mean(x1 * x1, axis=-1, keepdims=True)
    hf_ref[...] = (x1 * lax.rsqrt(ms + EPS) * nw_ref[...]).astype(BF16)


def _merge(y_ssd, o_att, o_mem, p2, x2, ws, wa, wm, wo, nw, tm):
    t = x2.shape[0]
    tok = lambda w, c=0: pl.BlockSpec((tm, w), lambda i: (i, c))
    res = lambda a: pl.BlockSpec(a.shape, lambda i: (0, 0))
    return pl.pallas_call(
        _merge_kernel,
        grid=(t // tm,),
        in_specs=[tok(SSD_INNER), tok(D_MODEL), tok(D_MODEL),
                  tok(D_MODEL, COL_G), tok(D_MODEL, COL_G + 1), tok(D_MODEL, COL_G + 2), tok(D_MODEL),
                  res(ws), res(wa), res(wm), res(wo), res(nw)],
        out_specs=[tok(D_MODEL), tok(D_MODEL)],
        out_shape=[jax.ShapeDtypeStruct((t, D_MODEL), F32), jax.ShapeDtypeStruct((t, D_MODEL), BF16)],
        compiler_params=_params(("parallel",)),
        name="merge",
    )(y_ssd, o_att, o_mem, p2, p2, p2, x2, ws, wa, wm, wo, nw)


def _top16(srcs, n_rows, tp):
    r_id = _iota((n_rows, tp), 0)
    slot = _iota((PEER_TOPK, tp), 0)

    def body(r, carry):
        out = []
        for src_scr, (vals, idxs) in zip(srcs, carry):
            sc = src_scr[...]
            m = jnp.max(sc, axis=0, keepdims=True)
            idx = jnp.min(jnp.where(sc == m, r_id, n_rows), axis=0, keepdims=True)
            src_scr[...] = jnp.where(r_id == idx, NEG_INF, sc)
            out.append((jnp.where(slot == r, m, vals), jnp.where(slot == r, idx, idxs)))
        return tuple(out)

    zero = (jnp.zeros((PEER_TOPK, tp), F32), jnp.zeros((PEER_TOPK, tp), I32))
    return lax.fori_loop(0, PEER_TOPK, body, tuple(zero for _ in srcs))


def _peer_sel_kernel(hf_ref, wqt_ref, keys_ref, a_ref, b_ref, g_ref,
                     qt_scr, sc_scr, sv_scr, si_scr, cand_scr, oa_scr, ob_scr, og_scr):
    tp = hf_ref.shape[0]
    nk = PEER_NKEYS
    kk = PEER_TOPK
    qt_scr[...] = _dot_nt(wqt_ref[...], hf_ref[...])

    def halves(h, carry):
        for c in range(2):
            hc = 2 * h + c
            q_hc = qt_scr[pl.ds(pl.multiple_of(hc * nk, nk), nk), :]
            sc_scr[c] = _dot3(keys_ref[hc], q_hc, _dot)
        tops = _top16([sc_scr.at[0], sc_scr.at[1]], nk, tp)
        for c in range(2):
            sv_scr[2 * h + c] = tops[c][0]
            si_scr[2 * h + c] = tops[c][1]
        return carry

    lax.fori_loop(0, PEER_HEADS, halves, 0)

    def head_pair(hp, carry):
        b8 = _iota((SUBLANES, tp), 0)
        for w in range(2):
            sv1 = sv_scr[4 * hp + 2 * w]
            sv2 = sv_scr[4 * hp + 2 * w + 1]
            cand = cand_scr.at[w]
            cand[0:kk, :] = sv1[0:1, :] + sv2
            for a in range(1, SUBLANES):
                c = sv1[a:a + 1, :] + sv2[0:SUBLANES, :]
                n_b = kk // (a + 1)
                cand[SUBLANES * (a + 1):SUBLANES * (a + 2), :] = c if n_b >= SUBLANES else jnp.where(b8 < n_b, c, NEG_INF)
            cand[PEER_NCAND - SUBLANES:, :] = sv1[SUBLANES:, :] + sv2[0:1, :]
        tops = _top16([cand_scr.at[0], cand_scr.at[1]], PEER_NCAND, tp)
        for w in range(2):
            fv, fr = tops[w]
            si1 = si_scr[4 * hp + 2 * w]
            si2 = si_scr[4 * hp + 2 * w + 1]
            mid = fr - kk
            fa = jnp.where(fr < kk, 0,
                           jnp.where(fr < PEER_NCAND - SUBLANES, 1 + (mid >> 3), fr - (PEER_NCAND - 2 * SUBLANES)))
            fb = jnp.where(fr < kk, fr, jnp.where(fr < PEER_NCAND - SUBLANES, mid & (SUBLANES - 1), 0))
            e1 = jnp.zeros((kk, tp), I32)
            e2 = jnp.zeros((kk, tp), I32)
            for a in range(kk):
                e1 = e1 + jnp.where(fa == a, si1[a:a + 1, :], 0)
                e2 = e2 + jnp.where(fb == a, si2[a:a + 1, :], 0)
            e = jnp.exp(fv - fv[0:1, :])
            rows = pl.ds(pl.multiple_of((2 * hp + w) * kk, kk), kk)
            oa_scr[rows, :] = e1.astype(F32)
            ob_scr[rows, :] = e2.astype(F32)
            og_scr[rows, :] = e / jnp.sum(e, axis=0, keepdims=True)
        return carry

    lax.fori_loop(0, PEER_HEADS // 2, head_pair, 0)
    a_ref[...] = oa_scr[...].T
    b_ref[...] = ob_scr[...].T
    g_ref[...] = og_scr[...].T


def _peer_sel(hf, wqt, keys, tp):
    t = hf.shape[0]
    kk = PEER_TOPK
    npick = PEER_HEADS * kk
    out = pl.BlockSpec((tp, npick), lambda i: (i, 0))
    return pl.pallas_call(
        _peer_sel_kernel,
        grid=(t // tp,),
        in_specs=[pl.BlockSpec((tp, D_MODEL), lambda i: (i, 0)),
                  pl.BlockSpec(wqt.shape, lambda i: (0, 0)),
                  pl.BlockSpec(keys.shape, lambda i: (0, 0, 0))],
        out_specs=[out, out, out],
        out_shape=[jax.ShapeDtypeStruct((t, npick), F32)] * 3,
        scratch_shapes=[pltpu.VMEM((wqt.shape[0], tp), F32),
                        pltpu.VMEM((2, PEER_NKEYS, tp), F32),
                        pltpu.VMEM((2 * PEER_HEADS, kk, tp), F32),
                        pltpu.VMEM((2 * PEER_HEADS, kk, tp), I32),
                        pltpu.VMEM((2, PEER_NCAND, tp), F32),
                        pltpu.VMEM((npick, tp), F32),
                        pltpu.VMEM((npick, tp), F32),
                        pltpu.VMEM((npick, tp), F32)],
        compiler_params=_params(("parallel",)),
        name="peer_sel",
    )(hf, wqt, keys)


def _peer_kernel(hf_ref, a_ref, b_ref, g_ref, x1_ref, u_ref, v_ref, o_ref, g_scr, *, tm, te):
    nk = PEER_NKEYS
    e = pl.program_id(1)
    n1 = te // nk

    @pl.when(e == 0)
    def _():
        o_ref[...] = x1_ref[...]
        sub = _iota((nk, LANES), 0).astype(F32)

        def toks(tg, carry):
            for u in range(PEER_BUILD_TOKENS):
                t = tg * PEER_BUILD_TOKENS + u
                row = pl.ds(t, 1)
                y = jnp.where(sub == a_ref[row, :], g_ref[row, :], 0.0).astype(BF16)
                z = jnp.where(sub == b_ref[row, :], 1.0, 0.0).astype(BF16)
                g_scr[pl.ds(pl.multiple_of(t * nk, nk), nk), :] = _dot_nt(y, z)
            return carry

        lax.fori_loop(0, tm // PEER_BUILD_TOKENS, toks, 0)

    hf = hf_ref[...]
    act = _dot_nt(hf, u_ref[...])
    act = 0.5 * act * (1.0 + lax.erf(act * (2.0 ** -0.5)))
    ws = []
    for k in range(n1):
        gk = g_scr[pl.ds(e * n1 + k, tm, stride=nk), :]
        ws.append((act[:, k * nk:(k + 1) * nk] * gk).astype(BF16))
    o_ref[...] += _dot(jnp.concatenate(ws, axis=1), v_ref[...])


def _peer(hf, a, b, g, x1, u, v, tm, te):
    t = hf.shape[0]
    n_exp = u.shape[0]
    npick = a.shape[1]
    tok = lambda w: pl.BlockSpec((tm, w), lambda i, e: (i, 0))
    exp = pl.BlockSpec((te, D_MODEL), lambda i, e: (e, 0))
    return pl.pallas_call(
        functools.partial(_peer_kernel, tm=tm, te=te),
        grid=(t // tm, n_exp // te),
        in_specs=[tok(D_MODEL), tok(npick), tok(npick), tok(npick), tok(D_MODEL), exp, exp],
        out_specs=tok(D_MODEL),
        out_shape=jax.ShapeDtypeStruct((t, D_MODEL), F32),
        scratch_shapes=[pltpu.VMEM((tm * PEER_NKEYS, PEER_NKEYS), F32)],
        compiler_params=_params(("parallel", "arbitrary")),
        name="peer",
    )(hf, a, b, g, x1, u, v)


def _pack_w_in(w_in):
    n0 = SSD_INNER + SSD_CONV_DIM
    w_main = jnp.concatenate([w_in[:, :n0], w_in[:, n0 + SSD_HEADS:]], axis=1).astype(BF16)
    w_dt = jnp.pad(w_in[:, n0:n0 + SSD_HEADS], ((0, 0), (0, LANES - SSD_HEADS))).astype(BF16)
    return w_main, w_dt


def _pad_lanes(v):
    return jnp.pad(v, (0, LANES - v.shape[0]))[None]


def kernel(x_prompt, x_sample, mem_prompt, cache_k, cache_v, page_table, cache_mem_k, cache_mem_v, state_ssm, state_conv, norm1_w, w_in, conv_w, conv_b, dt_bias, a_log, d_skip, ssd_norm_w, q_norm_w, k_norm_w, mem_norm_w, w_mem_k, w_mem_v, mem_qn_w, mem_kn_w, w_br_ssd, w_br_attn, w_br_mem, w_out, norm2_w, peer_wq, peer_keys, peer_u, peer_v):
    bp, sp, _ = x_prompt.shape
    bs, ss, _ = x_sample.shape
    assert ss == 1 and cache_k.shape[0] == 1
    tp, ts = bp * sp, bs * ss

    w_main, w_dt = _pack_w_in(w_in[0])
    nw1 = norm1_w[0][None]
    ssd_w = (conv_w[0], conv_b[0][None], _pad_lanes(dt_bias[0]), _pad_lanes(a_log[0]),
             jnp.repeat(d_skip[0], SSD_HEADDIM)[None], ssd_norm_w[0][None])
    qw2 = jnp.tile(q_norm_w[0], LANES // ATT_HEADDIM)[None]
    kw2 = jnp.tile(k_norm_w[0], LANES // ATT_HEADDIM)[None]
    w_mem_kv = jnp.concatenate([w_mem_k[0], w_mem_v[0]], axis=1).astype(BF16)
    merge_w = (w_br_ssd[0].astype(BF16), w_br_attn[0].astype(BF16), w_br_mem[0].astype(BF16),
               w_out[0].astype(BF16), norm2_w[0][None])
    wqt = peer_wq[0].T.astype(BF16)
    keys = peer_keys[0].reshape(2 * PEER_HEADS, PEER_NKEYS, -1)
    u_bf = peer_u[0].astype(BF16)
    v_bf = peer_v[0].astype(BF16)
    mem_qw = mem_qn_w[0][None]

    xp = x_prompt.reshape(tp, D_MODEL)
    p_main = _proj(xp, nw1, w_main, min(2048, tp), 1024)
    p_dt = _proj(xp, nw1, w_dt, 1024, LANES)
    p3 = p_main.reshape(bp, sp, N_MAIN)
    y_ssd, ssm_p = _ssd(p3, p_dt.reshape(bp, sp, LANES), jnp.zeros((bp, SUBLANES, SSD_CONV_DIM), F32),
                        jnp.zeros((bp, SSD_HEADS, SSD_HEADDIM, SSD_STATE), F32), *ssd_w, SSD_CHUNK)
    cos, sin = _rope_tables(jnp.arange(sp, dtype=I32))
    qa, k_rot, ka, v_p, vt, sel = _qk(p3, cos, sin, qw2, kw2)
    o_att = _attn(qa, ka, vt, sel)
    n_mem = mem_prompt.shape[1]
    kv = _proj(mem_prompt.reshape(bp * n_mem, D_MODEL), mem_norm_w[0][None], w_mem_kv, 256, D_MODEL)
    mem_k = _headnorm(kv, mem_kn_w[0][None], 256)
    mem_v = kv[:, D_MODEL:]
    o_mem = _mem_attend(p3, COL_MQ, mem_k.reshape(bp, n_mem, D_MODEL), mem_v.reshape(bp, n_mem, D_MODEL), mem_qw, 512)
    x1, hf = _merge(y_ssd.reshape(tp, SSD_INNER), o_att.reshape(tp, D_MODEL), o_mem.reshape(tp, D_MODEL),
                    p_main, xp, *merge_w, 512)
    pa, pb, pg = _peer_sel(hf, wqt, keys, 256)
    y_p = _peer(hf, pa, pb, pg, x1, u_bf, v_bf, 256, 2048)

    xs = x_sample.reshape(ts, D_MODEL)
    ps_main = _proj(xs, nw1, w_main, ts, 1024)
    ps_dt = _proj(xs, nw1, w_dt, ts, LANES)
    ps3 = ps_main.reshape(bs, 1, N_MAIN)
    conv_prev8 = jnp.pad(state_conv[0], ((0, 0), (SUBLANES - (SSD_CONV - 1), 0), (0, 0)))
    ys_ssd, ssm_s = _ssd(ps3, ps_dt.reshape(bs, 1, LANES), conv_prev8, state_ssm[0], *ssd_w, 1)
    cos_s, sin_s = _rope_tables(jnp.full((ts,), PAST_LEN, I32))
    q_rot_s, k_rot_s = _qk_s(ps_main, cos_s, sin_s, qw2, kw2)
    v_s = ps_main[:, COL_V * D_MODEL:(COL_V + 1) * D_MODEL]
    cols = lambda v: v.reshape(bs, ATT_HEADS, ATT_HEADDIM, 1)
    pool = lambda c: jnp.transpose(c[0], (0, 2, 3, 1))
    o_att_s = _dec_attn(cols(q_rot_s), cols(k_rot_s), cols(v_s), pool(cache_k), pool(cache_v), page_table)
    mq8 = jnp.broadcast_to(ps3[:, :, COL_MQ * D_MODEL:(COL_MQ + 1) * D_MODEL], (bs, SUBLANES, D_MODEL))
    o_mem_s = _mem_attend(mq8, 0, cache_mem_k[0].reshape(bs, n_mem, D_MODEL),
                          cache_mem_v[0].reshape(bs, n_mem, D_MODEL), mem_qw, SUBLANES)[:, 0]
    x1_s, hf_s = _merge(ys_ssd.reshape(ts, SSD_INNER), o_att_s.reshape(ts, D_MODEL).astype(BF16), o_mem_s,
                        ps_main, xs, *merge_w, ts)
    pad_rows = lambda v: jnp.pad(v, ((0, LANES - ts), (0, 0)))
    hf_sp = pad_rows(hf_s)
    sa, sb, sg = _peer_sel(hf_sp, wqt, keys, LANES)
    y_s = _peer(hf_sp, sa, sb, sg, pad_rows(x1_s), u_bf, v_bf, LANES, 2048)[:ts]

    kv_shape = (1, bp, sp, ATT_HEADS, ATT_HEADDIM)
    mem_shape = (1, bp, n_mem, MEM_HEADS, MEM_HEADDIM)
    xbc_cols = slice(COL_X * 1024, COL_X * 1024 + SSD_CONV_DIM)
    conv_p = p3[:, sp - (SSD_CONV - 1):, xbc_cols]
    conv_s = jnp.concatenate([state_conv[0][:, 1:], ps3[:, :, xbc_cols]], axis=1)
    return (y_p.reshape(bp, sp, D_MODEL), y_s.reshape(bs, ss, D_MODEL),
            k_rot.reshape(kv_shape), v_p.reshape(kv_shape),
            mem_k.reshape(mem_shape), mem_v.reshape(mem_shape),
            ssm_p[None], conv_p[None],
            k_rot_s.reshape(1, bs, ss, ATT_HEADS, ATT_HEADDIM), v_s.reshape(1, bs, ss, ATT_HEADS, ATT_HEADDIM),
            ssm_s[None], conv_s[None])
```
